```python
import math
import jax, jax.numpy as jnp
from jax import lax
import numpy as np

D_MODEL = 1024
BATCH = 32
SEQ = 2048
DEPTH = 1

CTX_LEN = 256
GRID_W = 64

N_ATTN_HEADS = 16
QK_NOPE_DIM = 64
QK_ROPE_DIM = 32
V_HEAD_DIM = 64
Q_LORA_RANK = 384
KV_LORA_RANK = 256
ROPE_THETA = 10000.0
Q_BLOCK = 128
ATTN_SCALE = (QK_NOPE_DIM + QK_ROPE_DIM) ** -0.5

N_SSD_HEADS = 16
SSD_HEAD_DIM = 64
SSD_GROUPS = 2
HEADS_PER_GROUP = N_SSD_HEADS // SSD_GROUPS
SSD_STATE = 128
SSD_CONV = 5
SSD_CHUNK = 128
D_INNER = N_SSD_HEADS * SSD_HEAD_DIM
GN = SSD_GROUPS * SSD_STATE
XBC_WIDTH = D_INNER + 2 * GN

ATTN_WIDTH = N_ATTN_HEADS * V_HEAD_DIM
MIX_WIDTH = ATTN_WIDTH + D_INNER

D_FF = 2816
FFN_CONV = 3

N_MOD = 6
EPS = 1e-6

IN_SPLITS = (Q_LORA_RANK, KV_LORA_RANK, QK_ROPE_DIM, D_INNER, XBC_WIDTH, 2 * N_SSD_HEADS)
IN_WIDTH = sum(IN_SPLITS)
IN_OFFSETS = tuple(int(o) for o in np.cumsum(IN_SPLITS)[:-1])

kernel_name = "hybrid_mla_ssd_diffusion_layer"


def rms_norm(x, w):
    xf = x.astype(jnp.float32)
    y = xf * lax.rsqrt(jnp.mean(xf * xf, axis=-1, keepdims=True) + EPS)
    return (y * w.astype(jnp.float32)).astype(x.dtype)


def modulate(h, shift, scale):
    return h * (1 + scale) + shift


def in_split(h):
    return jnp.split(h, IN_OFFSETS, axis=-1)


def depthwise_conv(x, w, b):
    k = w.shape[0]
    out = lax.conv_general_dilated(
        x, w[:, None, :].astype(x.dtype), window_strides=(1,), padding=((k // 2, k // 2),),
        dimension_numbers=("NWC", "WIO", "NWC"), feature_group_count=x.shape[-1])
    return out + b


def axial_rope_tables(seq_len):
    n_rows = seq_len // GRID_W
    row = jnp.repeat(jnp.arange(n_rows), GRID_W).astype(jnp.float32)
    col = jnp.tile(jnp.arange(GRID_W), n_rows).astype(jnp.float32)
    axis_dim = QK_ROPE_DIM // 2
    inv_freq = ROPE_THETA ** (-jnp.arange(0, axis_dim, 2, dtype=jnp.float32) / axis_dim)
    ang_r = row[:, None] * inv_freq
    ang_c = col[:, None] * inv_freq
    ang = jnp.concatenate([ang_r, ang_r, ang_c, ang_c], axis=-1)
    return jnp.cos(ang), jnp.sin(ang)


def rotate_half_axial(x):
    def rh(t):
        a, b = jnp.split(t, 2, axis=-1)
        return jnp.concatenate([-b, a], axis=-1)
    xr, xc = jnp.split(x, 2, axis=-1)
    return jnp.concatenate([rh(xr), rh(xc)], axis=-1)


def apply_rope(x, cos, sin):
    xf = x.astype(jnp.float32)
    return (xf * cos + rotate_half_axial(xf) * sin).astype(x.dtype)


def mla_queries(cq, q_norm_w, w_q_up):
    b, l, _ = cq.shape
    q = (rms_norm(cq, q_norm_w) @ w_q_up).reshape(b, l, N_ATTN_HEADS, QK_NOPE_DIM + QK_ROPE_DIM)
    return q[..., :QK_NOPE_DIM], q[..., QK_NOPE_DIM:]


def mla_keys_values(ckv, kv_norm_w, w_kv_up):
    b, l, _ = ckv.shape
    kv = (rms_norm(ckv, kv_norm_w) @ w_kv_up).reshape(b, l, N_ATTN_HEADS, QK_NOPE_DIM + V_HEAD_DIM)
    return kv[..., :QK_NOPE_DIM], kv[..., QK_NOPE_DIM:]


def attend(q_nope, q_rope, k_nope, k_rope, v):
    s = (jnp.einsum("bqhd,bkhd->bhqk", q_nope, k_nope)
         + jnp.einsum("bqhr,bkr->bhqk", q_rope, k_rope)) * ATTN_SCALE
    p = jax.nn.softmax(s.astype(jnp.float32), axis=-1).astype(v.dtype)
    return jnp.einsum("bhqk,bkhd->bqhd", p, v)


def latent_attention(q_nope, q_rope, k_nope, k_rope, v):
    b, s, h, _ = q_nope.shape
    nb = s // Q_BLOCK

    def to_blocks(t):
        return jnp.moveaxis(t.reshape(b, nb, Q_BLOCK, *t.shape[2:]), 1, 0)

    out = lax.map(lambda qb: attend(qb[0], qb[1], k_nope, k_rope, v),
                  (to_blocks(q_nope), to_blocks(q_rope)))
    return jnp.moveaxis(out, 0, 1).reshape(b, s, h * V_HEAD_DIM)


def ssd_prepare(xbc, dt_raw, conv_w, conv_b, dt_bias):
    b, l, _ = xbc.shape
    xbc = jax.nn.silu(depthwise_conv(xbc, conv_w, conv_b))
    xs, bm, cm = jnp.split(xbc, [D_INNER, D_INNER + GN], axis=-1)
    xs = xs.reshape(b, l, SSD_GROUPS, HEADS_PER_GROUP, SSD_HEAD_DIM)
    bm = bm.reshape(b, l, SSD_GROUPS, SSD_STATE)
    cm = cm.reshape(b, l, SSD_GROUPS, SSD_STATE)
    dt = jax.nn.softplus(dt_raw.astype(jnp.float32).reshape(b, l, 2, SSD_GROUPS, HEADS_PER_GROUP)
                         + dt_bias.astype(jnp.float32).reshape(2, SSD_GROUPS, HEADS_PER_GROUP))
    return xs, bm, cm, dt


def segment_decay(cum):
    n = cum.shape[-1]
    diff = cum[..., :, None] - cum[..., None, :]
    mask = jnp.tril(jnp.ones((n, n), dtype=bool))
    return jnp.exp(jnp.where(mask, diff, -jnp.inf))


def ssd_states(xh, dt, A, bm, cm, init_state):
    b, l, g, e, p = xh.shape
    nc = l // SSD_CHUNK
    xd = (xh * dt[..., None]).reshape(b, nc, SSD_CHUNK, g, e, p)
    a_cum = jnp.cumsum(jnp.moveaxis((dt * A).reshape(b, nc, SSD_CHUNK, g, e), 2, -1), axis=-1)
    bc = bm.reshape(b, nc, SSD_CHUNK, g, -1)
    cc = cm.reshape(b, nc, SSD_CHUNK, g, -1)
    decay_to_end = jnp.exp(a_cum[..., -1:] - a_cum)
    chunk_states = jnp.einsum("bclgn,bcgel,bclgep->bcgepn", bc, decay_to_end, xd)
    states = jnp.concatenate([init_state[:, None], chunk_states], axis=1)
    chunk_cum = jnp.cumsum(jnp.pad(a_cum[..., -1], ((0, 0), (1, 0), (0, 0), (0, 0))), axis=1)
    decay_chunks = segment_decay(jnp.moveaxis(chunk_cum, 1, -1))
    states = jnp.einsum("bgezc,bcgepn->bzgepn", decay_chunks, states)
    return (xd, a_cum, bc, cc), states[:, :-1], states[:, -1]


def ssd_output(pieces, entering_states):
    xd, a_cum, bc, cc = pieces
    within = segment_decay(a_cum)
    cb = jnp.einsum("bclgn,bcsgn->bcgls", cc, bc)
    y_diag = jnp.einsum("bcgls,bcgels,bcsgep->bclgep", cb, within, xd)
    y_off = jnp.einsum("bclgn,bcgepn,bcgel->bclgep", cc, entering_states, jnp.exp(a_cum))
    b, nc, q, g, e, p = y_diag.shape
    return (y_diag + y_off).reshape(b, nc * q, g, e, p)


def ssd_finish(y, xs, z, d_skip, norm_w):
    b, l = y.shape[:2]
    y = y + d_skip.reshape(SSD_GROUPS, HEADS_PER_GROUP, 1) * xs
    y = y.reshape(b, l, D_INNER).astype(z.dtype)
    return rms_norm(y * jax.nn.silu(z), norm_w)


def conv_glu(h, w_up, conv_w, conv_b, w_down):
    gate, val = jnp.split(h @ w_up, 2, axis=-1)
    gate = depthwise_conv(gate, conv_w, conv_b)
    return (jax.nn.gelu(gate, approximate=False) * val) @ w_down


def setup_inputs(seed: int = 0) -> dict:
    key = jax.random.key(seed)
    ks = jax.random.split(key, 32)
    f32 = jnp.float32
    L = DEPTH

    def dense(k, shape, fan_in):
        return jax.random.normal(k, shape, f32) * fan_in ** -0.5

    def gain(k, shape):
        return 1.0 + 0.1 * jax.random.normal(k, shape, f32)

    def bias(k, shape):
        return 0.02 * jax.random.normal(k, shape, f32)

    dt0 = jnp.exp(jax.random.uniform(ks[15], (L, 2, N_SSD_HEADS), f32, math.log(1e-3), math.log(1e-1)))
    return {
        "x": jax.random.normal(ks[0], (BATCH, SEQ, D_MODEL), f32),
        "c": jax.random.normal(ks[1], (BATCH, D_MODEL), f32),
        "ctx": jax.random.normal(ks[2], (BATCH, CTX_LEN, D_MODEL), f32),
        "c_ctx": jax.random.normal(ks[3], (D_MODEL,), f32),
        "w_mod": dense(ks[4], (L, D_MODEL, N_MOD * D_MODEL), D_MODEL),
        "b_mod": bias(ks[5], (L, N_MOD * D_MODEL)),
        "mix_pre_norm": gain(ks[6], (L, D_MODEL)),
        "mix_post_norm": gain(ks[7], (L, D_MODEL)),
        "w_in": dense(ks[8], (L, D_MODEL, IN_WIDTH), D_MODEL),
        "q_norm": gain(ks[9], (L, Q_LORA_RANK)),
        "w_q_up": dense(ks[10], (L, Q_LORA_RANK, N_ATTN_HEADS * (QK_NOPE_DIM + QK_ROPE_DIM)), Q_LORA_RANK),
        "kv_norm": gain(ks[11], (L, KV_LORA_RANK)),
        "w_kv_up": dense(ks[12], (L, KV_LORA_RANK, N_ATTN_HEADS * (QK_NOPE_DIM + V_HEAD_DIM)), KV_LORA_RANK),
        "ssd_conv_w": dense(ks[13], (L, SSD_CONV, XBC_WIDTH), SSD_CONV),
        "ssd_conv_b": bias(ks[14], (L, XBC_WIDTH)),
        "ssd_a_log": jnp.log(jax.random.uniform(ks[16], (L, 2, N_SSD_HEADS), f32, 1.0, 16.0)),
        "ssd_dt_bias": dt0 + jnp.log(-jnp.expm1(-dt0)),
        "ssd_d": gain(ks[17], (L, N_SSD_HEADS)),
        "ssd_norm": gain(ks[18], (L, D_INNER)),
        "w_out": dense(ks[19], (L, MIX_WIDTH, D_MODEL), MIX_WIDTH),
        "ffn_pre_norm": gain(ks[20], (L, D_MODEL)),
        "ffn_post_norm": gain(ks[21], (L, D_MODEL)),
        "w_up": dense(ks[22], (L, D_MODEL, 2 * D_FF), D_MODEL),
        "ffn_conv_w": dense(ks[23], (L, FFN_CONV, D_FF), FFN_CONV),
        "ffn_conv_b": bias(ks[24], (L, D_FF)),
        "w_down": dense(ks[25], (L, D_FF, D_MODEL), D_FF),
    }


def reference(x, c, ctx, c_ctx, w_mod, b_mod, mix_pre_norm, mix_post_norm, w_in, q_norm, w_q_up,
              kv_norm, w_kv_up, ssd_conv_w, ssd_conv_b, ssd_a_log, ssd_dt_bias, ssd_d, ssd_norm, w_out,
              ffn_pre_norm, ffn_post_norm, w_up, ffn_conv_w, ffn_conv_b, w_down):
    bsz, seq, _ = x.shape
    cos, sin = axial_rope_tables(seq)
    cos_q, sin_q = cos[:, None, :], sin[:, None, :]

    def flip(t):
        return jnp.flip(t, axis=1)

    for l in range(DEPTH):
        last = l == DEPTH - 1
        mod_x = jnp.split((jax.nn.silu(c) @ w_mod[l] + b_mod[l])[:, None, :], N_MOD, axis=-1)
        mod_c = jnp.split((jax.nn.silu(c_ctx) @ w_mod[l] + b_mod[l])[None, None, :], N_MOD, axis=-1)

        cq_x, ckv_x, kr_x, z_x, xbc_x, dt_x = in_split(
            modulate(rms_norm(x, mix_pre_norm[l]), mod_x[0], mod_x[1]) @ w_in[l])
        cq_c, ckv_c, kr_c, z_c, xbc_c, dt_c = in_split(
            modulate(rms_norm(ctx, mix_pre_norm[l]), mod_c[0], mod_c[1]) @ w_in[l])

        q_nope_x, q_rope_x = mla_queries(cq_x, q_norm[l], w_q_up[l])
        q_rope_x = apply_rope(q_rope_x, cos_q, sin_q)
        k_nope_x, v_x = mla_keys_values(ckv_x, kv_norm[l], w_kv_up[l])
        k_rope_x = apply_rope(kr_x, cos, sin)
        k_nope_c, v_c = mla_keys_values(ckv_c, kv_norm[l], w_kv_up[l])
        attn_x = latent_attention(q_nope_x, q_rope_x,
                                  jnp.concatenate([k_nope_c, k_nope_x], axis=1),
                                  jnp.concatenate([kr_c, k_rope_x], axis=1),
                                  jnp.concatenate([v_c, v_x], axis=1))

        A = -jnp.exp(ssd_a_log[l].astype(jnp.float32)).reshape(2, SSD_GROUPS, HEADS_PER_GROUP)
        xs_x, b_x, c_x, dtv_x = ssd_prepare(xbc_x, dt_x, ssd_conv_w[l], ssd_conv_b[l], ssd_dt_bias[l])
        xs_c, b_c, c_c, dtv_c = ssd_prepare(xbc_c, dt_c, ssd_conv_w[l], ssd_conv_b[l], ssd_dt_bias[l])
        zero_state = jnp.zeros((bsz, SSD_GROUPS, HEADS_PER_GROUP, SSD_HEAD_DIM, SSD_STATE), xs_x.dtype)
        st_cf = ssd_states(xs_c, dtv_c[:, :, 0], A[0], b_c, c_c, zero_state)
        st_cb = ssd_states(flip(xs_c), flip(dtv_c[:, :, 1]), A[1], flip(b_c), flip(c_c), zero_state)
        st_xf = ssd_states(xs_x, dtv_x[:, :, 0], A[0], b_x, c_x, st_cf[2])
        st_xb = ssd_states(flip(xs_x), flip(dtv_x[:, :, 1]), A[1], flip(b_x), flip(c_x), st_cb[2])
        y_x = ssd_output(st_xf[0], st_xf[1]) + flip(ssd_output(st_xb[0], st_xb[1]))
        ssd_x = ssd_finish(y_x, xs_x, z_x, ssd_d[l], ssd_norm[l])

        mix_x = jnp.concatenate([attn_x, ssd_x], axis=-1) @ w_out[l]
        x = x + mod_x[2] * rms_norm(mix_x, mix_post_norm[l])

        if not last:
            q_nope_c, q_rope_c = mla_queries(cq_c, q_norm[l], w_q_up[l])
            attn_c = attend(q_nope_c, q_rope_c, k_nope_c, kr_c, v_c).reshape(bsz, -1, ATTN_WIDTH)
            y_c = ssd_output(st_cf[0], st_cf[1]) + flip(ssd_output(st_cb[0], st_cb[1]))
            ssd_c = ssd_finish(y_c, xs_c, z_c, ssd_d[l], ssd_norm[l])
            mix_c = jnp.concatenate([attn_c, ssd_c], axis=-1) @ w_out[l]
            ctx = ctx + mod_c[2] * rms_norm(mix_c, mix_post_norm[l])
            ffn_c = conv_glu(modulate(rms_norm(ctx, ffn_pre_norm[l]), mod_c[3], mod_c[4]),
                             w_up[l], ffn_conv_w[l], ffn_conv_b[l], w_down[l])
            ctx = ctx + mod_c[5] * rms_norm(ffn_c, ffn_post_norm[l])

        ffn_x = conv_glu(modulate(rms_norm(x, ffn_pre_norm[l]), mod_x[3], mod_x[4]),
                         w_up[l], ffn_conv_w[l], ffn_conv_b[l], w_down[l])
        x = x + mod_x[5] * rms_norm(ffn_x, ffn_post_norm[l])
    return x
```

```python
import functools
import math

import jax
import jax.numpy as jnp
import numpy as np
from jax import lax
from jax.experimental import pallas as pl
from jax.experimental.pallas import tpu as pltpu

f32 = jnp.float32
bf16 = jnp.bfloat16

D_MODEL = 1024
GRID_W = 64
N_ATTN_HEADS = 16
QK_NOPE_DIM = 64
QK_ROPE_DIM = 32
V_HEAD_DIM = 64
Q_LORA_RANK = 384
KV_LORA_RANK = 256
ROPE_THETA = 10000.0
ATTN_SCALE = (QK_NOPE_DIM + QK_ROPE_DIM) ** -0.5
N_SSD_HEADS = 16
SSD_HEAD_DIM = 64
SSD_GROUPS = 2
HEADS_PER_GROUP = N_SSD_HEADS // SSD_GROUPS
SSD_STATE = 128
SSD_CONV = 5
SSD_CHUNK = 128
D_INNER = N_SSD_HEADS * SSD_HEAD_DIM
GN = SSD_GROUPS * SSD_STATE
XBC_WIDTH = D_INNER + 2 * GN
D_FF = 2816
FFN_CONV = 3
N_MOD = 6
EPS = 1e-6

LANES = 128
HEAD_PAD = 128
GROUP_W = HEADS_PER_GROUP * SSD_HEAD_DIM
VMEM_LIMIT = 56 * 1024 * 1024

_C_CQ = 0
_C_CKV = _C_CQ + Q_LORA_RANK
_C_KR = _C_CKV + KV_LORA_RANK
_C_Z = _C_KR + 2 * LANES
_C_XBC = _C_Z + D_INNER
_C_DT = _C_XBC + XBC_WIDTH
_C_END = _C_DT + SSD_GROUPS * LANES


def _cparams(sem):
    return pltpu.CompilerParams(dimension_semantics=sem, vmem_limit_bytes=VMEM_LIMIT)


def _rms(v, w):
    return v * lax.rsqrt(jnp.mean(v * v, axis=-1, keepdims=True) + EPS) * w


def _sigmoid(v):
    return 1.0 / (1.0 + jnp.exp(-v))


def _dot(a, b):
    return jnp.dot(a, b, preferred_element_type=f32)


def _dot_nt(a, b):
    return lax.dot_general(a, b, (((1,), (1,)), ((), ())), preferred_element_type=f32)


def _mod_body(c_ref, w_ref, b_ref, o_ref):
    c = c_ref[...]
    s = (c * _sigmoid(c)).astype(bf16)
    o_ref[...] = _dot(s, w_ref[...]) + b_ref[...]


def _modulation(cvec, w_mod, b_mod):
    rows = cvec.shape[0]
    n = w_mod.shape[1]
    bn = D_MODEL
    return pl.pallas_call(
        _mod_body,
        out_shape=jax.ShapeDtypeStruct((rows, n), f32),
        grid=(n // bn,),
        in_specs=[pl.BlockSpec((rows, D_MODEL), lambda j: (0, 0)),
                  pl.BlockSpec((D_MODEL, bn), lambda j: (0, j)),
                  pl.BlockSpec((1, bn), lambda j: (0, j))],
        out_specs=pl.BlockSpec((rows, bn), lambda j: (0, j)),
        compiler_params=_cparams(("arbitrary",)),
        name="modulation",
    )(cvec, w_mod, b_mod)


def _inproj_body(*refs, is_ctx):
    if is_ctx:
        (x_ref, mod_ref, pre_ref, w1_ref, kvn_ref, wk_ref, wv_ref,
         k_ref, v_ref, xbc_ref, dt_ref) = refs
    else:
        (x_ref, mod_ref, pre_ref, w1_ref, kvn_ref, wk_ref, wv_ref, qn_ref, wq_ref,
         tabq_ref, cosk_ref, sink_ref,
         q_ref, k_ref, v_ref, z_ref, xbc_ref, dt_ref) = refs
    x = x_ref[0]
    shift = mod_ref[0, 0:1, :]
    scale = mod_ref[0, 1:2, :]
    h = (_rms(x, pre_ref[...]) * (1.0 + scale) + shift).astype(bf16)

    ckv = _dot(h, w1_ref[:, _C_CKV:_C_KR])
    ckvn = _rms(ckv, kvn_ref[...]).astype(bf16)
    v_ref[0] = _dot(ckvn, wv_ref[...]).astype(bf16)
    kres = _dot(ckvn, wk_ref[...])
    if is_ctx:
        kk = _dot(h, w1_ref[:, _C_KR:_C_KR + LANES])
    else:
        kr2 = _dot(h, w1_ref[:, _C_KR:_C_Z])
        kk = kr2[:, :LANES] * cosk_ref[...] + kr2[:, LANES:] * sink_ref[...]
    for hd in range(N_ATTN_HEADS):
        sl = slice(hd * HEAD_PAD, (hd + 1) * HEAD_PAD)
        k_ref[0, :, sl] = (kres[:, sl] + kk).astype(bf16)

    if not is_ctx:
        cq = _dot(h, w1_ref[:, _C_CQ:_C_CKV])
        cqn = _rms(cq, qn_ref[...]).astype(bf16)
        qres = _dot(cqn, wq_ref[...])
        tab = tabq_ref[...]
        for hd in range(N_ATTN_HEADS):
            sl = slice(hd * HEAD_PAD, (hd + 1) * HEAD_PAD)
            q_ref[0, :, sl] = (qres[:, sl] * tab).astype(bf16)
        z_ref[0] = _dot(h, w1_ref[:, _C_Z:_C_XBC]).astype(bf16)

    xbc_ref[0] = _dot(h, w1_ref[:, _C_XBC:_C_DT]).astype(bf16)
    dt_ref[0] = _dot(h, w1_ref[:, _C_DT:_C_END])


def _inproj(x, mod, mod_row, pre, w1, kvn, wk, wv, latent_args, tm):
    bsz, length, _ = x.shape
    is_ctx = latent_args is None
    nt = length // tm
    full = lambda a: pl.BlockSpec(a.shape, lambda b, i: (0,) * a.ndim)
    row = lambda w: pl.BlockSpec((1, tm, w), lambda b, i: (b, i, 0))
    tab = lambda a: pl.BlockSpec((tm, a.shape[1]), lambda b, i: (i, 0))
    if mod_row is None:
        mod_spec = pl.BlockSpec((1, N_MOD, D_MODEL), lambda b, i: (b, 0, 0))
    else:
        mod_spec = pl.BlockSpec((1, N_MOD, D_MODEL), lambda b, i: (mod_row, 0, 0))
    in_specs = [row(D_MODEL), mod_spec, full(pre), full(w1), full(kvn), full(wk), full(wv)]
    args = [x, mod, pre, w1, kvn, wk, wv]
    kw = N_ATTN_HEADS * HEAD_PAD
    vw = N_ATTN_HEADS * V_HEAD_DIM
    dtw = SSD_GROUPS * LANES
    sds = lambda w, dt: jax.ShapeDtypeStruct((bsz, length, w), dt)
    if is_ctx:
        out_shape = (sds(kw, bf16), sds(vw, bf16), sds(XBC_WIDTH, bf16), sds(dtw, f32))
        out_specs = (row(kw), row(vw), row(XBC_WIDTH), row(dtw))
    else:
        qn, wq, tabq, cosk, sink = latent_args
        in_specs += [full(qn), full(wq), tab(tabq), tab(cosk), tab(sink)]
        args += [qn, wq, tabq, cosk, sink]
        out_shape = (sds(kw, bf16), sds(kw, bf16), sds(vw, bf16), sds(D_INNER, bf16),
                     sds(XBC_WIDTH, bf16), sds(dtw, f32))
        out_specs = (row(kw), row(kw), row(vw), row(D_INNER), row(XBC_WIDTH), row(dtw))
    return pl.pallas_call(
        functools.partial(_inproj_body, is_ctx=is_ctx),
        out_shape=out_shape,
        grid=(bsz, nt),
        in_specs=in_specs,
        out_specs=out_specs,
        compiler_params=_cparams(("parallel", "parallel")),
        name="inproj_ctx" if is_ctx else "inproj_latent",
    )(*args)


def _attn_body(q_ref, kx_ref, kc_ref, vx_ref, vc_ref, o_ref):
    tq = q_ref.shape[1]
    lane = lax.broadcasted_iota(jnp.int32, (tq, LANES), 1)
    outs = []
    for hh in range(2):
        sl = slice(hh * HEAD_PAD, (hh + 1) * HEAD_PAD)
        q = q_ref[0, :, sl]
        sx = _dot_nt(q, kx_ref[0, :, sl])
        sc = _dot_nt(q, kc_ref[0, :, sl])
        m = jnp.maximum(jnp.max(sx, axis=-1, keepdims=True), jnp.max(sc, axis=-1, keepdims=True))
        px = jnp.exp(sx - m)
        pc = jnp.exp(sc - m)
        l = jnp.sum(px, axis=-1, keepdims=True) + jnp.sum(pc, axis=-1, keepdims=True)
        o = _dot(px.astype(bf16), vx_ref[0]) + _dot(pc.astype(bf16), vc_ref[0])
        outs.append(o / l)
    o_ref[0] = jnp.where(lane < V_HEAD_DIM, outs[0], outs[1]).astype(bf16)


def _attention(q, kx, kc, vx, vc, tq):
    bsz, s, _ = q.shape
    lc = kc.shape[1]
    npair = N_ATTN_HEADS // 2
    return pl.pallas_call(
        _attn_body,
        out_shape=jax.ShapeDtypeStruct((bsz, s, N_ATTN_HEADS * V_HEAD_DIM), bf16),
        grid=(bsz, npair, s // tq),
        in_specs=[pl.BlockSpec((1, tq, 2 * HEAD_PAD), lambda b, p, i: (b, i, p)),
                  pl.BlockSpec((1, s, 2 * HEAD_PAD), lambda b, p, i: (b, 0, p)),
                  pl.BlockSpec((1, lc, 2 * HEAD_PAD), lambda b, p, i: (b, 0, p)),
                  pl.BlockSpec((1, s, 2 * V_HEAD_DIM), lambda b, p, i: (b, 0, p)),
                  pl.BlockSpec((1, lc, 2 * V_HEAD_DIM), lambda b, p, i: (b, 0, p))],
        out_specs=pl.BlockSpec((1, tq, 2 * V_HEAD_DIM), lambda b, p, i: (b, i, p)),
        compiler_params=_cparams(("parallel", "parallel", "arbitrary")),
        name="attention",
    )(q, kx, kc, vx, vc)


_HALO = 16


def _conv_silu_pass(src_ref, w_ref, b_ref, dst_ref, length):
    q = SSD_CHUNK
    nc = length // q
    half = SSD_CONV // 2
    w = w_ref[...]
    bias = b_ref[...]

    def step(c, carry):
        t0 = pl.multiple_of(c * q, q)
        tp = pl.multiple_of(jnp.maximum(t0 - _HALO, 0), _HALO)
        tn = pl.multiple_of(jnp.minimum(t0 + q, length - _HALO), _HALO)
        prev = src_ref[0, pl.ds(tp, _HALO), :].astype(f32)
        nxt = src_ref[0, pl.ds(tn, _HALO), :].astype(f32)
        prev = jnp.where(c > 0, prev, 0.0)
        nxt = jnp.where(c < nc - 1, nxt, 0.0)
        win = jnp.concatenate([prev, src_ref[0, pl.ds(t0, q), :].astype(f32), nxt], axis=0)
        acc = bias + w[0:1, :] * win[_HALO - half:_HALO - half + q, :]
        for k in range(1, SSD_CONV):
            o = _HALO - half + k
            acc = acc + w[k:k + 1, :] * win[o:o + q, :]
        dst_ref[pl.ds(t0, q), :] = acc * _sigmoid(acc)
        return carry

    lax.fori_loop(0, nc, step, 0)


def _softplus(v):
    return jnp.maximum(v, 0.0) + jnp.log1p(jnp.exp(-jnp.abs(v)))


def _split3_dot(m_bf, a):
    a1 = a.astype(bf16)
    r1 = a - a1.astype(f32)
    a2 = r1.astype(bf16)
    a3 = (r1 - a2.astype(f32)).astype(bf16)
    return _dot(m_bf, a1) + _dot(m_bf, a2) + _dot(m_bf, a3)


def _ssd_chunk(xs, bm, cm, dt_raw, dt_bias, a_row, st_ref, d, want_y):
    q = SSD_CHUNK
    ri = lax.broadcasted_iota(jnp.int32, (q, q), 0)
    ci = lax.broadcasted_iota(jnp.int32, (q, q), 1)
    mask = (ci <= ri) if d == 0 else (ci >= ri)
    lane = lax.broadcasted_iota(jnp.int32, (q, LANES), 1)
    first_half = lane < SSD_HEAD_DIM
    first_half_row = lax.broadcasted_iota(jnp.int32, (1, LANES), 1) < SSD_HEAD_DIM

    dt = _softplus(dt_raw + dt_bias)
    a = dt * a_row
    tri = jnp.where(mask, 1.0, 0.0).astype(bf16)
    acum = _split3_dot(tri, a)
    total = acum[q - 1:q, :] if d == 0 else acum[0:1, :]
    etot = jnp.exp(total)
    bt = bm.T.astype(bf16)
    if want_y:
        at = acum.T
        cb = _dot_nt(cm.astype(bf16), bm.astype(bf16))
    ys = []
    for p in range(HEADS_PER_GROUP // 2):
        psl = slice(p * LANES, (p + 1) * LANES)
        lhs, dtc, dec, etc = [], [], [], []
        for e in (2 * p, 2 * p + 1):
            j = d * HEADS_PER_GROUP + e
            col = acum[:, j:j + 1]
            if want_y:
                row = at[j:j + 1, :]
                seg = jnp.exp(jnp.where(mask, col - row, -jnp.inf))
                lhs.append(jnp.concatenate([(cb * seg).astype(bf16),
                                            (cm * jnp.exp(col)).astype(bf16)], axis=1))
            dtc.append(dt[:, j:j + 1])
            dec.append(jnp.exp(total[:, j:j + 1] - col))
            etc.append(etot[:, j:j + 1])
        xd = xs[:, psl] * jnp.where(first_half, dtc[0], dtc[1])
        st = st_ref[:, psl]
        if want_y:
            rhs = jnp.concatenate([xd.astype(bf16), st.astype(bf16)], axis=0)
            ys.append(jnp.where(first_half, _dot(lhs[0], rhs), _dot(lhs[1], rhs)))
        xdec = (xd * jnp.where(first_half, dec[0], dec[1])).astype(bf16)
        st_ref[:, psl] = jnp.where(first_half_row, etc[0], etc[1]) * st + _dot(bt, xdec)
    if want_y:
        return jnp.concatenate(ys, axis=1)
    return None


def _ssd_body(xs_x, b_x, c_x, xs_c, b_c, c_c, dt_x, dt_c, cw_xs, cw_b, cw_c, cb_xs, cb_b, cb_c,
              dtb_ref, alog_ref, dexp_ref, y_ref,
              xs_s, b_s, c_s, xsc_s, bc_s, cc_s, y_s, st_ref):
    s = xs_x.shape[1]
    lc = xs_c.shape[1]
    q = SSD_CHUNK
    nc, ncc = s // q, lc // q
    _conv_silu_pass(xs_x, cw_xs, cb_xs, xs_s, s)
    _conv_silu_pass(b_x, cw_b, cb_b, b_s, s)
    _conv_silu_pass(c_x, cw_c, cb_c, c_s, s)
    _conv_silu_pass(xs_c, cw_xs, cb_xs, xsc_s, lc)
    _conv_silu_pass(b_c, cw_b, cb_b, bc_s, lc)
    _conv_silu_pass(c_c, cw_c, cb_c, cc_s, lc)

    dt_bias = dtb_ref[...]
    a_row = -jnp.exp(alog_ref[...])
    dexp = dexp_ref[...]

    for d in range(2):
        st_ref[...] = jnp.zeros(st_ref.shape, f32)

        def ctx_step(i, carry, d=d):
            c = i if d == 0 else ncc - 1 - i
            rows = pl.ds(pl.multiple_of(c * q, q), q)
            _ssd_chunk(xsc_s[rows, :], bc_s[rows, :], cc_s[rows, :], dt_c[0, rows, :],
                       dt_bias, a_row, st_ref, d, False)
            return carry

        lax.fori_loop(0, ncc, ctx_step, 0)

        def lat_step(i, carry, d=d):
            c = i if d == 0 else nc - 1 - i
            rows = pl.ds(pl.multiple_of(c * q, q), q)
            xs = xs_s[rows, :]
            y = _ssd_chunk(xs, b_s[rows, :], c_s[rows, :], dt_x[0, rows, :],
                           dt_bias, a_row, st_ref, d, True)
            if d == 0:
                y_s[rows, :] = y
            else:
                y_ref[0, rows, :] = (y_s[rows, :] + y + dexp * xs).astype(bf16)
            return carry

        lax.fori_loop(0, nc, lat_step, 0)


def _ssd(xbc_x, xbc_c, dt_x, dt_c, conv_w, conv_b, dtb, alog, dexp):
    bsz, s, _ = xbc_x.shape
    lc = xbc_c.shape[1]
    nxb = D_INNER // GROUP_W
    b_blk = D_INNER // SSD_STATE
    c_blk = (D_INNER + GN) // SSD_STATE
    seq = lambda length, w, blk: pl.BlockSpec((1, length, w), lambda b, g: (b, 0, blk(g)))
    par = lambda rows, w, blk: pl.BlockSpec((rows, w), lambda b, g: (0, blk(g)))
    xs_i = lambda g: g
    b_i = lambda g: b_blk + g
    c_i = lambda g: c_blk + g
    in_specs = [seq(s, GROUP_W, xs_i), seq(s, SSD_STATE, b_i), seq(s, SSD_STATE, c_i),
                seq(lc, GROUP_W, xs_i), seq(lc, SSD_STATE, b_i), seq(lc, SSD_STATE, c_i),
                seq(s, LANES, xs_i), seq(lc, LANES, xs_i),
                par(SSD_CONV, GROUP_W, xs_i), par(SSD_CONV, SSD_STATE, b_i), par(SSD_CONV, SSD_STATE, c_i),
                par(1, GROUP_W, xs_i), par(1, SSD_STATE, b_i), par(1, SSD_STATE, c_i),
                par(1, LANES, xs_i), par(1, LANES, xs_i), par(1, GROUP_W, xs_i)]
    assert nxb == SSD_GROUPS
    scratch = [pltpu.VMEM((s, GROUP_W), f32), pltpu.VMEM((s, SSD_STATE), f32), pltpu.VMEM((s, SSD_STATE), f32),
               pltpu.VMEM((lc, GROUP_W), f32), pltpu.VMEM((lc, SSD_STATE), f32), pltpu.VMEM((lc, SSD_STATE), f32),
               pltpu.VMEM((s, GROUP_W), f32), pltpu.VMEM((SSD_STATE, GROUP_W), f32)]
    return pl.pallas_call(
        _ssd_body,
        out_shape=jax.ShapeDtypeStruct((bsz, s, D_INNER), bf16),
        grid=(bsz, SSD_GROUPS),
        in_specs=in_specs,
        out_specs=pl.BlockSpec((1, s, GROUP_W), lambda b, g: (b, 0, g)),
        scratch_shapes=scratch,
        compiler_params=_cparams(("parallel", "parallel")),
        name="ssd_scan",
    )(xbc_x, xbc_x, xbc_x, xbc_c, xbc_c, xbc_c, dt_x, dt_c, conv_w, conv_w, conv_w,
      conv_b, conv_b, conv_b, dtb, alog, dexp)


def _outproj_body(attn_ref, y_ref, z_ref, x_ref, mod_ref, wa_ref, ws_ref, sn_ref, pn_ref, o_ref):
    z = z_ref[0].astype(f32)
    g = y_ref[0].astype(f32) * (z * _sigmoid(z))
    gn = _rms(g, sn_ref[...]).astype(bf16)
    mix = _dot(attn_ref[0], wa_ref[...]) + _dot(gn, ws_ref[...])
    o_ref[0] = x_ref[0] + mod_ref[0, 2:3, :] * _rms(mix, pn_ref[...])


def _outproj(attn, y, z, x, mod, wa, ws, sn, pn, tm):
    bsz, s, _ = x.shape
    full = lambda a: pl.BlockSpec(a.shape, lambda b, i: (0,) * a.ndim)
    row = lambda w: pl.BlockSpec((1, tm, w), lambda b, i: (b, i, 0))
    return pl.pallas_call(
        _outproj_body,
        out_shape=jax.ShapeDtypeStruct((bsz, s, D_MODEL), f32),
        grid=(bsz, s // tm),
        in_specs=[row(attn.shape[2]), row(D_INNER), row(D_INNER), row(D_MODEL),
                  pl.BlockSpec((1, N_MOD, D_MODEL), lambda b, i: (b, 0, 0)),
                  full(wa), full(ws), full(sn), full(pn)],
        out_specs=row(D_MODEL),
        compiler_params=_cparams(("parallel", "parallel")),
        name="outproj",
    )(attn, y, z, x, mod, wa, ws, sn, pn)


_FFN_ROWS = 512
_FFN_PAD = 8


def _ffn_body(x_ref, mod_ref, pre_ref, wg_ref, wv_ref, cw_ref, cb_ref, wd_ref, pn_ref, o_ref, h_s, g_s):
    j = pl.program_id(1)
    nj = pl.num_programs(1)
    s = x_ref.shape[1]
    tf = wg_ref.shape[1]
    rb = _FFN_ROWS
    nrb = s // rb

    @pl.when(j == 0)
    def _():
        shift = mod_ref[0, 3:4, :]
        scale = mod_ref[0, 4:5, :]
        pre = pre_ref[...]

        def hstep(r, carry):
            rows = pl.ds(pl.multiple_of(r * rb, rb), rb)
            h_s[rows, :] = (_rms(x_ref[0, rows, :], pre) * (1.0 + scale) + shift).astype(bf16)
            o_ref[0, rows, :] = jnp.zeros((rb, D_MODEL), f32)
            return carry

        lax.fori_loop(0, nrb, hstep, 0)
        g_s[0:_FFN_PAD, :] = jnp.zeros((_FFN_PAD, tf), f32)
        g_s[_FFN_PAD + s:2 * _FFN_PAD + s, :] = jnp.zeros((_FFN_PAD, tf), f32)

    def gstep(r, carry):
        r0 = pl.multiple_of(r * rb, rb)
        g_s[pl.ds(r0 + _FFN_PAD, rb), :] = _dot(h_s[pl.ds(r0, rb), :], wg_ref[...])
        return carry

    lax.fori_loop(0, nrb, gstep, 0)

    cw = cw_ref[...]
    cb = cb_ref[...]

    def dstep(r, carry):
        r0 = pl.multiple_of(r * rb, rb)
        rows = pl.ds(r0, rb)
        win = g_s[pl.ds(r0, rb + 2 * _FFN_PAD), :]
        gc = cb + cw[0:1, :] * win[_FFN_PAD - 1:_FFN_PAD - 1 + rb, :]
        gc = gc + cw[1:2, :] * win[_FFN_PAD:_FFN_PAD + rb, :]
        gc = gc + cw[2:3, :] * win[_FFN_PAD + 1:_FFN_PAD + 1 + rb, :]
        val = _dot(h_s[rows, :], wv_ref[...])
        act = (0.5 * gc * (1.0 + lax.erf(gc * (1.0 / math.sqrt(2.0)))) * val).astype(bf16)
        o_ref[0, rows, :] = o_ref[0, rows, :] + _dot(act, wd_ref[...])
        return carry

    lax.fori_loop(0, nrb, dstep, 0)

    @pl.when(j == nj - 1)
    def _():
        gate = mod_ref[0, 5:6, :]
        pn = pn_ref[...]

        def fstep(r, carry):
            rows = pl.ds(pl.multiple_of(r * rb, rb), rb)
            o_ref[0, rows, :] = x_ref[0, rows, :] + gate * _rms(o_ref[0, rows, :], pn)
            return carry

        lax.fori_loop(0, nrb, fstep, 0)


def _ffn(x1, mod, pre, wg, wv, cw, cb, wd, pn, tf):
    bsz, s, _ = x1.shape
    nj = D_FF // tf
    return pl.pallas_call(
        _ffn_body,
        out_shape=jax.ShapeDtypeStruct((bsz, s, D_MODEL), f32),
        grid=(bsz, nj),
        in_specs=[pl.BlockSpec((1, s, D_MODEL), lambda b, j: (b, 0, 0)),
                  pl.BlockSpec((1, N_MOD, D_MODEL), lambda b, j: (b, 0, 0)),
                  pl.BlockSpec((1, D_MODEL), lambda b, j: (0, 0)),
                  pl.BlockSpec((D_MODEL, tf), lambda b, j: (0, j)),
                  pl.BlockSpec((D_MODEL, tf), lambda b, j: (0, j)),
                  pl.BlockSpec((FFN_CONV, tf), lambda b, j: (0, j)),
                  pl.BlockSpec((1, tf), lambda b, j: (0, j)),
                  pl.BlockSpec((tf, D_MODEL), lambda b, j: (j, 0)),
                  pl.BlockSpec((1, D_MODEL), lambda b, j: (0, 0))],
        out_specs=pl.BlockSpec((1, s, D_MODEL), lambda b, j: (b, 0, 0)),
        scratch_shapes=[pltpu.VMEM((s, D_MODEL), bf16), pltpu.VMEM((s + 2 * _FFN_PAD, tf), f32)],
        compiler_params=_cparams(("parallel", "arbitrary")),
        name="convglu_ffn",
    )(x1, mod, pre, wg, wv, cw, cb, wd, pn)


def _rotate_half_axial(t):
    def rh(u):
        a, b = jnp.split(u, 2, axis=-1)
        return jnp.concatenate([-b, a], axis=-1)
    tr, tc = jnp.split(t, 2, axis=-1)
    return jnp.concatenate([rh(tr), rh(tc)], axis=-1)


def _rope_tables(seq_len):
    n_rows = seq_len // GRID_W
    row = jnp.repeat(jnp.arange(n_rows), GRID_W).astype(f32)
    col = jnp.tile(jnp.arange(GRID_W), n_rows).astype(f32)
    axis_dim = QK_ROPE_DIM // 2
    inv_freq = ROPE_THETA ** (-jnp.arange(0, axis_dim, 2, dtype=f32) / axis_dim)
    ang_r = row[:, None] * inv_freq
    ang_c = col[:, None] * inv_freq
    ang = jnp.concatenate([ang_r, ang_r, ang_c, ang_c], axis=-1)
    return jnp.cos(ang), jnp.sin(ang)


def _group_lanes(v):
    v = v.reshape(2, SSD_GROUPS, HEADS_PER_GROUP).transpose(1, 0, 2).reshape(SSD_GROUPS, 2 * HEADS_PER_GROUP)
    v = jnp.pad(v, ((0, 0), (0, LANES - 2 * HEADS_PER_GROUP)))
    return v.reshape(1, SSD_GROUPS * LANES)


def kernel(x, c, ctx, c_ctx, w_mod, b_mod, mix_pre_norm, mix_post_norm, w_in, q_norm, w_q_up, kv_norm, w_kv_up, ssd_conv_w, ssd_conv_b, ssd_a_log, ssd_dt_bias, ssd_d, ssd_norm, w_out, ffn_pre_norm, ffn_post_norm, w_up, ffn_conv_w, ffn_conv_b, w_down):
    bsz, seq, _ = x.shape
    assert w_mod.shape[0] == 1, "single-layer stack: the context stream is never updated"
    l = 0
    row2 = lambda v: v.reshape(1, -1)

    mod_rows = -(-(bsz + 1) // 8) * 8
    cvec = jnp.zeros((mod_rows, D_MODEL), f32).at[:bsz].set(c).at[bsz].set(c_ctx)

    wi = w_in[l]
    o_cq, o_ckv, o_kr, o_z, o_xbc, o_dt = np.cumsum([0, Q_LORA_RANK, KV_LORA_RANK, QK_ROPE_DIM, D_INNER, XBC_WIDTH])
    w_kr = wi[:, o_kr:o_z]
    w_kr_rot = _rotate_half_axial(w_kr)
    zpad = jnp.zeros((D_MODEL, QK_NOPE_DIM), f32)
    w_dt = wi[:, o_dt:].reshape(D_MODEL, 2, SSD_GROUPS, HEADS_PER_GROUP).transpose(0, 2, 1, 3)
    w_dt = jnp.pad(w_dt.reshape(D_MODEL, SSD_GROUPS, 2 * HEADS_PER_GROUP),
                   ((0, 0), (0, 0), (0, LANES - 2 * HEADS_PER_GROUP))).reshape(D_MODEL, SSD_GROUPS * LANES)
    w1 = jnp.concatenate([wi[:, o_cq:o_kr], zpad, w_kr, w_kr, zpad, w_kr_rot, w_kr_rot,
                          wi[:, o_z:o_dt], w_dt], axis=1).astype(bf16)
    assert w1.shape[1] == _C_END

    wq3 = w_q_up[l].reshape(Q_LORA_RANK, N_ATTN_HEADS, QK_NOPE_DIM + QK_ROPE_DIM)
    wq_rope = wq3[..., QK_NOPE_DIM:]
    wq = jnp.concatenate([wq3[..., :QK_NOPE_DIM], wq_rope, _rotate_half_axial(wq_rope)], axis=-1)
    wq = wq.reshape(Q_LORA_RANK, N_ATTN_HEADS * HEAD_PAD).astype(bf16)
    wkv3 = w_kv_up[l].reshape(KV_LORA_RANK, N_ATTN_HEADS, QK_NOPE_DIM + V_HEAD_DIM)
    wk = jnp.pad(wkv3[..., :QK_NOPE_DIM], ((0, 0), (0, 0), (0, HEAD_PAD - QK_NOPE_DIM)))
    wk = wk.reshape(KV_LORA_RANK, N_ATTN_HEADS * HEAD_PAD).astype(bf16)
    wv = wkv3[..., QK_NOPE_DIM:].reshape(KV_LORA_RANK, N_ATTN_HEADS * V_HEAD_DIM).astype(bf16)

    cos, sin = _rope_tables(seq)
    ones = jnp.ones((seq, QK_NOPE_DIM), f32)
    zeros = jnp.zeros((seq, QK_NOPE_DIM), f32)
    tabq = jnp.concatenate([ones, cos, sin], axis=1) * ATTN_SCALE
    cosk = jnp.concatenate([zeros, cos, cos], axis=1)
    sink = jnp.concatenate([zeros, sin, sin], axis=1)

    dtb = _group_lanes(ssd_dt_bias[l])
    alog = _group_lanes(ssd_a_log[l])
    dexp = jnp.repeat(ssd_d[l], SSD_HEAD_DIM).reshape(1, D_INNER)

    wo = w_out[l].astype(bf16)
    wup = w_up[l].astype(bf16)
    wg, wval = wup[:, :D_FF], wup[:, D_FF:]
    wd = w_down[l].astype(bf16)

    mod = _modulation(cvec, w_mod[l].astype(bf16), row2(b_mod[l])).reshape(mod_rows, N_MOD, D_MODEL)
    pre = row2(mix_pre_norm[l])
    kvn = row2(kv_norm[l])
    q, kx, vx, z, xbc_x, dt_x = _inproj(x, mod, None, pre, w1, kvn, wk, wv,
                                       (row2(q_norm[l]), wq, tabq, cosk, sink), tm=512)
    kc, vc, xbc_c, dt_c = _inproj(ctx, mod, bsz, pre, w1, kvn, wk, wv, None, tm=ctx.shape[1])
    attn = _attention(q, kx, kc, vx, vc, tq=256)
    y = _ssd(xbc_x, xbc_c, dt_x, dt_c, ssd_conv_w[l], row2(ssd_conv_b[l]), dtb, alog, dexp)
    x1 = _outproj(attn, y, z, x, mod, wo[:N_ATTN_HEADS * V_HEAD_DIM], wo[N_ATTN_HEADS * V_HEAD_DIM:],
                  row2(ssd_norm[l]), row2(mix_post_norm[l]), tm=512)
    return _ffn(x1, mod, row2(ffn_pre_norm[l]), wg, wval, ffn_conv_w[l], row2(ffn_conv_b[l]), wd,
                row2(ffn_post_norm[l]), tf=256)
```

```python
import functools
import math

import jax
import jax.numpy as jnp
import numpy as np
from jax import lax
from jax.experimental import pallas as pl
from jax.experimental.pallas import tpu as pltpu

f32 = jnp.float32
bf16 = jnp.bfloat16

D_MODEL = 1024
GRID_W = 64
N_ATTN_HEADS = 16
QK_NOPE_DIM = 64
QK_ROPE_DIM = 32
V_HEAD_DIM = 64
Q_LORA_RANK = 384
KV_LORA_RANK = 256
ROPE_THETA = 10000.0
ATTN_SCALE = (QK_NOPE_DIM + QK_ROPE_DIM) ** -0.5
N_SSD_HEADS = 16
SSD_HEAD_DIM = 64
SSD_GROUPS = 2
HEADS_PER_GROUP = N_SSD_HEADS // SSD_GROUPS
SSD_STATE = 128
SSD_CONV = 5
SSD_CHUNK = 128
D_INNER = N_SSD_HEADS * SSD_HEAD_DIM
GN = SSD_GROUPS * SSD_STATE
XBC_WIDTH = D_INNER + 2 * GN
D_FF = 2816
FFN_CONV = 3
N_MOD = 6
EPS = 1e-6

LANES = 128
HEAD_PAD = 128
GROUP_W = HEADS_PER_GROUP * SSD_HEAD_DIM
VMEM_LIMIT = 56 * 1024 * 1024

_C_CQ = 0
_C_CKV = _C_CQ + Q_LORA_RANK
_C_KR = _C_CKV + KV_LORA_RANK
_C_Z = _C_KR + 2 * LANES
_C_XBC = _C_Z + D_INNER
_C_DT = _C_XBC + XBC_WIDTH
_C_END = _C_DT + SSD_GROUPS * LANES


def _cparams(sem):
    return pltpu.CompilerParams(dimension_semantics=sem, vmem_limit_bytes=VMEM_LIMIT)


def _rms(v, w):
    return v * lax.rsqrt(jnp.mean(v * v, axis=-1, keepdims=True) + EPS) * w


def _sigmoid(v):
    return 1.0 / (1.0 + jnp.exp(-v))


def _dot(a, b):
    return jnp.dot(a, b, preferred_element_type=f32)


def _dot_nt(a, b):
    return lax.dot_general(a, b, (((1,), (1,)), ((), ())), preferred_element_type=f32)


def _mod_body(c_ref, w_ref, b_ref, o_ref):
    c = c_ref[...]
    s = (c * _sigmoid(c)).astype(bf16)
    o_ref[...] = _dot(s, w_ref[...]) + b_ref[...]


def _modulation(cvec, w_mod, b_mod):
    rows = cvec.shape[0]
    n = w_mod.shape[1]
    bn = D_MODEL
    return pl.pallas_call(
        _mod_body,
        out_shape=jax.ShapeDtypeStruct((rows, n), f32),
        grid=(n // bn,),
        in_specs=[pl.BlockSpec((rows, D_MODEL), lambda j: (0, 0)),
                  pl.BlockSpec((D_MODEL, bn), lambda j: (0, j)),
                  pl.BlockSpec((1, bn), lambda j: (0, j))],
        out_specs=pl.BlockSpec((rows, bn), lambda j: (0, j)),
        compiler_params=_cparams(("arbitrary",)),
        name="modulation",
    )(cvec, w_mod, b_mod)


def _inproj_body(*refs, is_ctx):
    if is_ctx:
        (x_ref, mod_ref, pre_ref, w1_ref, kvn_ref, wk_ref, wv_ref,
         k_ref, v_ref, xbc_ref, dt_ref) = refs
    else:
        (x_ref, mod_ref, pre_ref, w1_ref, kvn_ref, wk_ref, wv_ref, qn_ref, wq_ref,
         tabq_ref, cosk_ref, sink_ref,
         q_ref, k_ref, v_ref, z_ref, xbc_ref, dt_ref) = refs
    x = x_ref[0]
    shift = mod_ref[0, 0:1, :]
    scale = mod_ref[0, 1:2, :]
    h = (_rms(x, pre_ref[...]) * (1.0 + scale) + shift).astype(bf16)

    ckv = _dot(h, w1_ref[:, _C_CKV:_C_KR])
    ckvn = _rms(ckv, kvn_ref[...]).astype(bf16)
    v_ref[0] = _dot(ckvn, wv_ref[...]).astype(bf16)
    kres = _dot(ckvn, wk_ref[...])
    if is_ctx:
        kk = _dot(h, w1_ref[:, _C_KR:_C_KR + LANES])
    else:
        kr2 = _dot(h, w1_ref[:, _C_KR:_C_Z])
        kk = kr2[:, :LANES] * cosk_ref[...] + kr2[:, LANES:] * sink_ref[...]
    for hd in range(N_ATTN_HEADS):
        sl = slice(hd * HEAD_PAD, (hd + 1) * HEAD_PAD)
        k_ref[0, :, sl] = (kres[:, sl] + kk).astype(bf16)

    if not is_ctx:
        cq = _dot(h, w1_ref[:, _C_CQ:_C_CKV])
        cqn = _rms(cq, qn_ref[...]).astype(bf16)
        qres = _dot(cqn, wq_ref[...])
        tab = tabq_ref[...]
        for hd in range(N_ATTN_HEADS):
            sl = slice(hd * HEAD_PAD, (hd + 1) * HEAD_PAD)
            q_ref[0, :, sl] = (qres[:, sl] * tab).astype(bf16)
        z_ref[0] = _dot(h, w1_ref[:, _C_Z:_C_XBC]).astype(bf16)

    xbc_ref[0] = _dot(h, w1_ref[:, _C_XBC:_C_DT]).astype(bf16)
    dt_ref[0] = _dot(h, w1_ref[:, _C_DT:_C_END])


def _inproj(x, mod, mod_row, pre, w1, kvn, wk, wv, latent_args, tm):
    bsz, length, _ = x.shape
    is_ctx = latent_args is None
    nt = length // tm
    full = lambda a: pl.BlockSpec(a.shape, lambda b, i: (0,) * a.ndim)
    row = lambda w: pl.BlockSpec((1, tm, w), lambda b, i: (b, i, 0))
    tab = lambda a: pl.BlockSpec((tm, a.shape[1]), lambda b, i: (i, 0))
    if mod_row is None:
        mod_spec = pl.BlockSpec((1, N_MOD, D_MODEL), lambda b, i: (b, 0, 0))
    else:
        mod_spec = pl.BlockSpec((1, N_MOD, D_MODEL), lambda b, i: (mod_row, 0, 0))
    in_specs = [row(D_MODEL), mod_spec, full(pre), full(w1), full(kvn), full(wk), full(wv)]
    args = [x, mod, pre, w1, kvn, wk, wv]
    kw = N_ATTN_HEADS * HEAD_PAD
    vw = N_ATTN_HEADS * V_HEAD_DIM
    dtw = SSD_GROUPS * LANES
    sds = lambda w, dt: jax.ShapeDtypeStruct((bsz, length, w), dt)
    if is_ctx:
        out_shape = (sds(kw, bf16), sds(vw, bf16), sds(XBC_WIDTH, bf16), sds(dtw, f32))
        out_specs = (row(kw), row(vw), row(XBC_WIDTH), row(dtw))
    else:
        qn, wq, tabq, cosk, sink = latent_args
        in_specs += [full(qn), full(wq), tab(tabq), tab(cosk), tab(sink)]
        args += [qn, wq, tabq, cosk, sink]
        out_shape = (sds(kw, bf16), sds(kw, bf16), sds(vw, bf16), sds(D_INNER, bf16),
                     sds(XBC_WIDTH, bf16), sds(dtw, f32))
        out_specs = (row(kw), row(kw), row(vw), row(D_INNER), row(XBC_WIDTH), row(dtw))
    return pl.pallas_call(
        functools.partial(_inproj_body, is_ctx=is_ctx),
        out_shape=out_shape,
        grid=(bsz, nt),
        in_specs=in_specs,
        out_specs=out_specs,
        compiler_params=_cparams(("parallel", "parallel")),
        name="inproj_ctx" if is_ctx else "inproj_latent",
    )(*args)


_ATTN_TQ = 256


def _attn_body(q_ref, kx_ref, kc_ref, vx_ref, vc_ref, o_ref, s0, s1, p0, p1, vxa, vca):
    tq = _ATTN_TQ
    s = q_ref.shape[1]
    lc = kc_ref.shape[1]
    nt = s // tq
    lane_v = lax.broadcasted_iota(jnp.int32, (1, LANES), 1)
    own = [lane_v < V_HEAD_DIM, lane_v >= V_HEAD_DIM]
    for hh in range(2):
        vxa[hh] = jnp.where(own[hh], vx_ref[0].astype(f32), 1.0).astype(bf16)
        vca[hh] = jnp.where(own[hh], vc_ref[0].astype(f32), 1.0).astype(bf16)

    def rows_of(t):
        return pl.ds(pl.multiple_of(t * tq, tq), tq)

    def scores(t, s_ref):
        rows = rows_of(t)
        for hh in range(2):
            sl = slice(hh * HEAD_PAD, (hh + 1) * HEAD_PAD)
            q = q_ref[0, rows, sl]
            s_ref[hh, :, 0:lc] = _dot_nt(q, kc_ref[0, :, sl])
            s_ref[hh, :, lc:lc + s] = _dot_nt(q, kx_ref[0, :, sl])

    def exps(s_ref, p_ref):
        strip = 16
        for hh in range(2):
            for r in range(0, tq, strip):
                sc = s_ref[hh, r:r + strip, :]
                m = jnp.max(sc, axis=-1, keepdims=True)
                p_ref[hh, r:r + strip, :] = jnp.exp2(sc - m).astype(bf16)

    def outputs(t, p_ref):
        res = []
        for hh in range(2):
            o = _dot(p_ref[hh, :, 0:lc], vca[hh]) + _dot(p_ref[hh, :, lc:lc + s], vxa[hh])
            den = o[:, V_HEAD_DIM:V_HEAD_DIM + 1] if hh == 0 else o[:, 0:1]
            res.append(o / den)
        o_ref[0, rows_of(t), :] = jnp.where(own[0], res[0], res[1]).astype(bf16)

    s_slot = (s0, s1)
    p_slot = (p0, p1)
    scores(0, s_slot[0])
    exps(s_slot[0], p_slot[0])
    scores(1, s_slot[1])

    def half(t, cur):
        outputs(t - 2, p_slot[cur])
        exps(s_slot[1 - cur], p_slot[1 - cur])
        scores(t, s_slot[cur])

    def steady(k, carry):
        t = 2 + 2 * k
        half(t, 0)
        pl.delay(1)
        half(t + 1, 1)
        return carry

    lax.fori_loop(0, (nt - 2) // 2, steady, 0)
    last = (nt - 1) % 2
    outputs(nt - 2, p_slot[1 - last])
    exps(s_slot[last], p_slot[last])
    outputs(nt - 1, p_slot[last])


def _attention(q, kx, kc, vx, vc):
    bsz, s, _ = q.shape
    lc = kc.shape[1]
    npair = N_ATTN_HEADS // 2
    tq = _ATTN_TQ
    assert s % (2 * tq) == 0 and s // tq >= 2
    seq = lambda length, w: pl.BlockSpec((1, length, w), lambda b, p: (b, 0, p))
    return pl.pallas_call(
        _attn_body,
        out_shape=jax.ShapeDtypeStruct((bsz, s, N_ATTN_HEADS * V_HEAD_DIM), bf16),
        grid=(bsz, npair),
        in_specs=[seq(s, 2 * HEAD_PAD), seq(s, 2 * HEAD_PAD), seq(lc, 2 * HEAD_PAD),
                  seq(s, 2 * V_HEAD_DIM), seq(lc, 2 * V_HEAD_DIM)],
        out_specs=seq(s, 2 * V_HEAD_DIM),
        scratch_shapes=[pltpu.VMEM((2, tq, lc + s), f32), pltpu.VMEM((2, tq, lc + s), f32),
                        pltpu.VMEM((2, tq, lc + s), bf16), pltpu.VMEM((2, tq, lc + s), bf16),
                        pltpu.VMEM((2, s, 2 * V_HEAD_DIM), bf16), pltpu.VMEM((2, lc, 2 * V_HEAD_DIM), bf16)],
        compiler_params=_cparams(("parallel", "parallel")),
        name="attention",
    )(q, kx, kc, vx, vc)


_HALO = 16


def _conv_silu_pass(src_ref, w_ref, b_ref, dst_ref, length):
    q = SSD_CHUNK
    nc = length // q
    half = SSD_CONV // 2
    w = w_ref[...]
    bias = b_ref[...]

    def step(c, carry):
        t0 = pl.multiple_of(c * q, q)
        tp = pl.multiple_of(jnp.maximum(t0 - _HALO, 0), _HALO)
        tn = pl.multiple_of(jnp.minimum(t0 + q, length - _HALO), _HALO)
        prev = src_ref[0, pl.ds(tp, _HALO), :].astype(f32)
        nxt = src_ref[0, pl.ds(tn, _HALO), :].astype(f32)
        prev = jnp.where(c > 0, prev, 0.0)
        nxt = jnp.where(c < nc - 1, nxt, 0.0)
        win = jnp.concatenate([prev, src_ref[0, pl.ds(t0, q), :].astype(f32), nxt], axis=0)
        acc = bias + w[0:1, :] * win[_HALO - half:_HALO - half + q, :]
        for k in range(1, SSD_CONV):
            o = _HALO - half + k
            acc = acc + w[k:k + 1, :] * win[o:o + q, :]
        dst_ref[pl.ds(t0, q), :] = acc * _sigmoid(acc)
        return carry

    lax.fori_loop(0, nc, step, 0)


def _softplus(v):
    return jnp.maximum(v, 0.0) + jnp.log1p(jnp.exp(-jnp.abs(v)))


def _split3_dot(m_bf, a):
    a1 = a.astype(bf16)
    r1 = a - a1.astype(f32)
    a2 = r1.astype(bf16)
    a3 = (r1 - a2.astype(f32)).astype(bf16)
    return _dot(m_bf, a1) + _dot(m_bf, a2) + _dot(m_bf, a3)


def _split2(a):
    a1 = a.astype(bf16)
    a2 = (a - a1.astype(f32)).astype(bf16)
    return jnp.concatenate([a1, a2], axis=1)


def _ssd_chunk(xs, bm, cm, dt_raw, dt_bias, a_row, st_ref, sel_head, sel_half, d, want_y):
    q = SSD_CHUNK
    ri = lax.broadcasted_iota(jnp.int32, (q, q), 0)
    ci = lax.broadcasted_iota(jnp.int32, (q, q), 1)
    mask = (ci <= ri) if d == 0 else (ci >= ri)
    first_half = lax.broadcasted_iota(jnp.int32, (q, LANES), 1) < SSD_HEAD_DIM
    edge = q - 1 if d == 0 else 0

    dt = _softplus(dt_raw + dt_bias)
    a = dt * a_row
    tri = jnp.where(mask, 1.0, 0.0).astype(bf16)
    acum = _split3_dot(tri, a)
    a2 = _split2(acum)
    dtb = _dot(_split2(dt), sel_half)
    bt = bm.T.astype(bf16)
    if want_y:
        colb = _dot(a2, sel_head)
        at = acum.T
        cb = _dot_nt(cm.astype(bf16), bm.astype(bf16))
    else:
        col_half = _dot(a2, sel_half)
    ys = []
    for p in range(HEADS_PER_GROUP // 2):
        psl = slice(p * LANES, (p + 1) * LANES)
        xd = xs[:, psl] * dtb[:, psl]
        st = st_ref[:, psl]
        if want_y:
            lhs, cols = [], []
            for e in (2 * p, 2 * p + 1):
                j = d * HEADS_PER_GROUP + e
                col = colb[:, e * LANES:(e + 1) * LANES]
                seg = jnp.exp(jnp.where(mask, col - at[j:j + 1, :], -jnp.inf))
                lhs.append(jnp.concatenate([(cb * seg).astype(bf16),
                                            (cm * jnp.exp(col)).astype(bf16)], axis=1))
                cols.append(col)
            col2 = jnp.where(first_half, cols[0], cols[1])
            rhs = jnp.concatenate([xd.astype(bf16), st.astype(bf16)], axis=0)
            ys.append(jnp.where(first_half, _dot(lhs[0], rhs), _dot(lhs[1], rhs)))
        else:
            col2 = col_half[:, psl]
        tot = col2[edge:edge + 1, :]
        xdec = (xd * jnp.exp(tot - col2)).astype(bf16)
        st_ref[:, psl] = jnp.exp(tot) * st + _dot(bt, xdec)
    if want_y:
        return jnp.concatenate(ys, axis=1)
    return None


def _ssd_body(xs_x, b_x, c_x, xs_c, b_c, c_c, dt_x, dt_c, cw_xs, cw_b, cw_c, cb_xs, cb_b, cb_c,
              dtb_ref, alog_ref, dexp_ref, selh_ref, selq_ref, y_ref,
              xs_s, b_s, c_s, xsc_s, bc_s, cc_s, y_s, stf_ref, stb_ref):
    s = xs_x.shape[1]
    lc = xs_c.shape[1]
    q = SSD_CHUNK
    nc, ncc = s // q, lc // q
    _conv_silu_pass(xs_x, cw_xs, cb_xs, xs_s, s)
    _conv_silu_pass(b_x, cw_b, cb_b, b_s, s)
    _conv_silu_pass(c_x, cw_c, cb_c, c_s, s)
    _conv_silu_pass(xs_c, cw_xs, cb_xs, xsc_s, lc)
    _conv_silu_pass(b_c, cw_b, cb_b, bc_s, lc)
    _conv_silu_pass(c_c, cw_c, cb_c, cc_s, lc)

    dt_bias = dtb_ref[...]
    a_row = -jnp.exp(alog_ref[...])
    dexp = dexp_ref[...]
    st_refs = (stf_ref, stb_ref)
    for st_ref in st_refs:
        st_ref[...] = jnp.zeros(st_ref.shape, f32)

    def chunk_rows(i, n, d):
        c = i if d == 0 else n - 1 - i
        return pl.ds(pl.multiple_of(c * q, q), q)

    def ctx_step(i, carry):
        for d in range(2):
            rows = chunk_rows(i, ncc, d)
            _ssd_chunk(xsc_s[rows, :], bc_s[rows, :], cc_s[rows, :], dt_c[0, rows, :],
                       dt_bias, a_row, st_refs[d], selh_ref[d], selq_ref[d], d, False)
        return carry

    lax.fori_loop(0, ncc, ctx_step, 0)

    def skip_step(c, carry):
        rows = pl.ds(pl.multiple_of(c * q, q), q)
        y_s[rows, :] = dexp * xs_s[rows, :]
        return carry

    lax.fori_loop(0, nc, skip_step, 0)

    def lat_step(i, carry):
        for d in range(2):
            rows = chunk_rows(i, nc, d)
            y = _ssd_chunk(xs_s[rows, :], b_s[rows, :], c_s[rows, :], dt_x[0, rows, :],
                           dt_bias, a_row, st_refs[d], selh_ref[d], selq_ref[d], d, True)
            y_s[rows, :] = y_s[rows, :] + y
        return carry

    lax.fori_loop(0, nc, lat_step, 0)

    def out_step(c, carry):
        rows = pl.ds(pl.multiple_of(c * q, q), q)
        y_ref[0, rows, :] = y_s[rows, :].astype(bf16)
        return carry

    lax.fori_loop(0, nc, out_step, 0)


def _lane_select(width):
    sel = np.zeros((2, 2 * LANES, HEADS_PER_GROUP * width), np.float32)
    for d in range(2):
        for k in range(2):
            for e in range(HEADS_PER_GROUP):
                sel[d, k * LANES + d * HEADS_PER_GROUP + e, e * width:(e + 1) * width] = 1.0
    return jnp.asarray(sel, dtype=bf16)


def _ssd(xbc_x, xbc_c, dt_x, dt_c, conv_w, conv_b, dtb, alog, dexp):
    bsz, s, _ = xbc_x.shape
    lc = xbc_c.shape[1]
    selh = _lane_select(LANES)
    selq = _lane_select(SSD_HEAD_DIM)
    nxb = D_INNER // GROUP_W
    b_blk = D_INNER // SSD_STATE
    c_blk = (D_INNER + GN) // SSD_STATE
    seq = lambda length, w, blk: pl.BlockSpec((1, length, w), lambda b, g: (b, 0, blk(g)))
    par = lambda rows, w, blk: pl.BlockSpec((rows, w), lambda b, g: (0, blk(g)))
    xs_i = lambda g: g
    b_i = lambda g: b_blk + g
    c_i = lambda g: c_blk + g
    in_specs = [seq(s, GROUP_W, xs_i), seq(s, SSD_STATE, b_i), seq(s, SSD_STATE, c_i),
                seq(lc, GROUP_W, xs_i), seq(lc, SSD_STATE, b_i), seq(lc, SSD_STATE, c_i),
                seq(s, LANES, xs_i), seq(lc, LANES, xs_i),
                par(SSD_CONV, GROUP_W, xs_i), par(SSD_CONV, SSD_STATE, b_i), par(SSD_CONV, SSD_STATE, c_i),
                par(1, GROUP_W, xs_i), par(1, SSD_STATE, b_i), par(1, SSD_STATE, c_i),
                par(1, LANES, xs_i), par(1, LANES, xs_i), par(1, GROUP_W, xs_i),
                pl.BlockSpec(selh.shape, lambda b, g: (0, 0, 0)),
                pl.BlockSpec(selq.shape, lambda b, g: (0, 0, 0))]
    assert nxb == SSD_GROUPS
    scratch = [pltpu.VMEM((s, GROUP_W), f32), pltpu.VMEM((s, SSD_STATE), f32), pltpu.VMEM((s, SSD_STATE), f32),
               pltpu.VMEM((lc, GROUP_W), f32), pltpu.VMEM((lc, SSD_STATE), f32), pltpu.VMEM((lc, SSD_STATE), f32),
               pltpu.VMEM((s, GROUP_W), f32),
               pltpu.VMEM((SSD_STATE, GROUP_W), f32), pltpu.VMEM((SSD_STATE, GROUP_W), f32)]
    return pl.pallas_call(
        _ssd_body,
        out_shape=jax.ShapeDtypeStruct((bsz, s, D_INNER), bf16),
        grid=(bsz, SSD_GROUPS),
        in_specs=in_specs,
        out_specs=pl.BlockSpec((1, s, GROUP_W), lambda b, g: (b, 0, g)),
        scratch_shapes=scratch,
        compiler_params=_cparams(("parallel", "parallel")),
        name="ssd_scan",
    )(xbc_x, xbc_x, xbc_x, xbc_c, xbc_c, xbc_c, dt_x, dt_c, conv_w, conv_w, conv_w,
      conv_b, conv_b, conv_b, dtb, alog, dexp, selh, selq)


def _outproj_body(attn_ref, y_ref, z_ref, x_ref, mod_ref, wa_ref, ws_ref, sn_ref, pn_ref, o_ref):
    z = z_ref[0].astype(f32)
    g = y_ref[0].astype(f32) * (z * _sigmoid(z))
    gn = _rms(g, sn_ref[...]).astype(bf16)
    mix = _dot(attn_ref[0], wa_ref[...]) + _dot(gn, ws_ref[...])
    o_ref[0] = x_ref[0] + mod_ref[0, 2:3, :] * _rms(mix, pn_ref[...])


def _outproj(attn, y, z, x, mod, wa, ws, sn, pn, tm):
    bsz, s, _ = x.shape
    full = lambda a: pl.BlockSpec(a.shape, lambda b, i: (0,) * a.ndim)
    row = lambda w: pl.BlockSpec((1, tm, w), lambda b, i: (b, i, 0))
    return pl.pallas_call(
        _outproj_body,
        out_shape=jax.ShapeDtypeStruct((bsz, s, D_MODEL), f32),
        grid=(bsz, s // tm),
        in_specs=[row(attn.shape[2]), row(D_INNER), row(D_INNER), row(D_MODEL),
                  pl.BlockSpec((1, N_MOD, D_MODEL), lambda b, i: (b, 0, 0)),
                  full(wa), full(ws), full(sn), full(pn)],
        out_specs=row(D_MODEL),
        compiler_params=_cparams(("parallel", "parallel")),
        name="outproj",
    )(attn, y, z, x, mod, wa, ws, sn, pn)


_FFN_ROWS = 512
_FFN_PAD = 8


def _ffn_body(x_ref, mod_ref, pre_ref, wg_ref, wv_ref, cw_ref, cb_ref, wd_ref, pn_ref, o_ref, h_s, g_s):
    j = pl.program_id(1)
    nj = pl.num_programs(1)
    s = x_ref.shape[1]
    tf = wg_ref.shape[1]
    rb = _FFN_ROWS
    nrb = s // rb

    @pl.when(j == 0)
    def _():
        shift = mod_ref[0, 3:4, :]
        scale = mod_ref[0, 4:5, :]
        pre = pre_ref[...]

        def hstep(r, carry):
            rows = pl.ds(pl.multiple_of(r * rb, rb), rb)
            h_s[rows, :] = (_rms(x_ref[0, rows, :], pre) * (1.0 + scale) + shift).astype(bf16)
            o_ref[0, rows, :] = jnp.zeros((rb, D_MODEL), f32)
            return carry

        lax.fori_loop(0, nrb, hstep, 0)
        g_s[0:_FFN_PAD, :] = jnp.zeros((_FFN_PAD, tf), f32)
        g_s[_FFN_PAD + s:2 * _FFN_PAD + s, :] = jnp.zeros((_FFN_PAD, tf), f32)

    for r in range(nrb):
        r0 = r * rb
        g_s[r0 + _FFN_PAD:r0 + _FFN_PAD + rb, :] = _dot(h_s[r0:r0 + rb, :], wg_ref[...])

    cw = cw_ref[...]
    cb = cb_ref[...]
    for r in range(nrb):
        r0 = r * rb
        win = g_s[r0:r0 + rb + 2 * _FFN_PAD, :]
        gc = cb + cw[0:1, :] * win[_FFN_PAD - 1:_FFN_PAD - 1 + rb, :]
        gc = gc + cw[1:2, :] * win[_FFN_PAD:_FFN_PAD + rb, :]
        gc = gc + cw[2:3, :] * win[_FFN_PAD + 1:_FFN_PAD + 1 + rb, :]
        val = _dot(h_s[r0:r0 + rb, :], wv_ref[...])
        act = (0.5 * gc * (1.0 + lax.erf(gc * (1.0 / math.sqrt(2.0)))) * val).astype(bf16)
        o_ref[0, r0:r0 + rb, :] = o_ref[0, r0:r0 + rb, :] + _dot(act, wd_ref[...])

    @pl.when(j == nj - 1)
    def _():
        gate = mod_ref[0, 5:6, :]
        pn = pn_ref[...]

        def fstep(r, carry):
            rows = pl.ds(pl.multiple_of(r * rb, rb), rb)
            o_ref[0, rows, :] = x_ref[0, rows, :] + gate * _rms(o_ref[0, rows, :], pn)
            return carry

        lax.fori_loop(0, nrb, fstep, 0)


def _ffn(x1, mod, pre, wg, wv, cw, cb, wd, pn, tf):
    bsz, s, _ = x1.shape
    nj = D_FF // tf
    return pl.pallas_call(
        _ffn_body,
        out_shape=jax.ShapeDtypeStruct((bsz, s, D_MODEL), f32),
        grid=(bsz, nj),
        in_specs=[pl.BlockSpec((1, s, D_MODEL), lambda b, j: (b, 0, 0)),
                  pl.BlockSpec((1, N_MOD, D_MODEL), lambda b, j: (b, 0, 0)),
                  pl.BlockSpec((1, D_MODEL), lambda b, j: (0, 0)),
                  pl.BlockSpec((D_MODEL, tf), lambda b, j: (0, j)),
                  pl.BlockSpec((D_MODEL, tf), lambda b, j: (0, j)),
                  pl.BlockSpec((FFN_CONV, tf), lambda b, j: (0, j)),
                  pl.BlockSpec((1, tf), lambda b, j: (0, j)),
                  pl.BlockSpec((tf, D_MODEL), lambda b, j: (j, 0)),
                  pl.BlockSpec((1, D_MODEL), lambda b, j: (0, 0))],
        out_specs=pl.BlockSpec((1, s, D_MODEL), lambda b, j: (b, 0, 0)),
        scratch_shapes=[pltpu.VMEM((s, D_MODEL), bf16), pltpu.VMEM((s + 2 * _FFN_PAD, tf), f32)],
        compiler_params=_cparams(("parallel", "arbitrary")),
        name="convglu_ffn",
    )(x1, mod, pre, wg, wv, cw, cb, wd, pn)


def _rotate_half_axial(t):
    def rh(u):
        a, b = jnp.split(u, 2, axis=-1)
        return jnp.concatenate([-b, a], axis=-1)
    tr, tc = jnp.split(t, 2, axis=-1)
    return jnp.concatenate([rh(tr), rh(tc)], axis=-1)


def _rope_tables(seq_len):
    n_rows = seq_len // GRID_W
    row = jnp.repeat(jnp.arange(n_rows), GRID_W).astype(f32)
    col = jnp.tile(jnp.arange(GRID_W), n_rows).astype(f32)
    axis_dim = QK_ROPE_DIM // 2
    inv_freq = ROPE_THETA ** (-jnp.arange(0, axis_dim, 2, dtype=f32) / axis_dim)
    ang_r = row[:, None] * inv_freq
    ang_c = col[:, None] * inv_freq
    ang = jnp.concatenate([ang_r, ang_r, ang_c, ang_c], axis=-1)
    return jnp.cos(ang), jnp.sin(ang)


def _group_lanes(v):
    v = v.reshape(2, SSD_GROUPS, HEADS_PER_GROUP).transpose(1, 0, 2).reshape(SSD_GROUPS, 2 * HEADS_PER_GROUP)
    v = jnp.pad(v, ((0, 0), (0, LANES - 2 * HEADS_PER_GROUP)))
    return v.reshape(1, SSD_GROUPS * LANES)


def kernel(x, c, ctx, c_ctx, w_mod, b_mod, mix_pre_norm, mix_post_norm, w_in, q_norm, w_q_up, kv_norm, w_kv_up, ssd_conv_w, ssd_conv_b, ssd_a_log, ssd_dt_bias, ssd_d, ssd_norm, w_out, ffn_pre_norm, ffn_post_norm, w_up, ffn_conv_w, ffn_conv_b, w_down):
    bsz, seq, _ = x.shape
    assert w_mod.shape[0] == 1, "single-layer stack: the context stream is never updated"
    l = 0
    row2 = lambda v: v.reshape(1, -1)

    mod_rows = -(-(bsz + 1) // 8) * 8
    cvec = jnp.zeros((mod_rows, D_MODEL), f32).at[:bsz].set(c).at[bsz].set(c_ctx)

    wi = w_in[l]
    o_cq, o_ckv, o_kr, o_z, o_xbc, o_dt = np.cumsum([0, Q_LORA_RANK, KV_LORA_RANK, QK_ROPE_DIM, D_INNER, XBC_WIDTH])
    w_kr = wi[:, o_kr:o_z]
    w_kr_rot = _rotate_half_axial(w_kr)
    zpad = jnp.zeros((D_MODEL, QK_NOPE_DIM), f32)
    w_dt = wi[:, o_dt:].reshape(D_MODEL, 2, SSD_GROUPS, HEADS_PER_GROUP).transpose(0, 2, 1, 3)
    w_dt = jnp.pad(w_dt.reshape(D_MODEL, SSD_GROUPS, 2 * HEADS_PER_GROUP),
                   ((0, 0), (0, 0), (0, LANES - 2 * HEADS_PER_GROUP))).reshape(D_MODEL, SSD_GROUPS * LANES)
    w1 = jnp.concatenate([wi[:, o_cq:o_kr], zpad, w_kr, w_kr, zpad, w_kr_rot, w_kr_rot,
                          wi[:, o_z:o_dt], w_dt], axis=1).astype(bf16)
    assert w1.shape[1] == _C_END

    wq3 = w_q_up[l].reshape(Q_LORA_RANK, N_ATTN_HEADS, QK_NOPE_DIM + QK_ROPE_DIM)
    wq_rope = wq3[..., QK_NOPE_DIM:]
    wq = jnp.concatenate([wq3[..., :QK_NOPE_DIM], wq_rope, _rotate_half_axial(wq_rope)], axis=-1)
    wq = wq.reshape(Q_LORA_RANK, N_ATTN_HEADS * HEAD_PAD).astype(bf16)
    wkv3 = w_kv_up[l].reshape(KV_LORA_RANK, N_ATTN_HEADS, QK_NOPE_DIM + V_HEAD_DIM)
    wk = jnp.pad(wkv3[..., :QK_NOPE_DIM], ((0, 0), (0, 0), (0, HEAD_PAD - QK_NOPE_DIM)))
    wk = wk.reshape(KV_LORA_RANK, N_ATTN_HEADS * HEAD_PAD).astype(bf16)
    wv = wkv3[..., QK_NOPE_DIM:].reshape(KV_LORA_RANK, N_ATTN_HEADS * V_HEAD_DIM).astype(bf16)

    cos, sin = _rope_tables(seq)
    ones = jnp.ones((seq, QK_NOPE_DIM), f32)
    zeros = jnp.zeros((seq, QK_NOPE_DIM), f32)
    tabq = jnp.concatenate([ones, cos, sin], axis=1) * (ATTN_SCALE * math.log2(math.e))
    cosk = jnp.concatenate([zeros, cos, cos], axis=1)
    sink = jnp.concatenate([zeros, sin, sin], axis=1)

    dtb = _group_lanes(ssd_dt_bias[l])
    alog = _group_lanes(ssd_a_log[l])
    dexp = jnp.repeat(ssd_d[l], SSD_HEAD_DIM).reshape(1, D_INNER)

    wo = w_out[l].astype(bf16)
    wup = w_up[l].astype(bf16)
    wg, wval = wup[:, :D_FF], wup[:, D_FF:]
    wd = w_down[l].astype(bf16)

    mod = _modulation(cvec, w_mod[l].astype(bf16), row2(b_mod[l])).reshape(mod_rows, N_MOD, D_MODEL)
    pre = row2(mix_pre_norm[l])
    kvn = row2(kv_norm[l])
    q, kx, vx, z, xbc_x, dt_x = _inproj(x, mod, None, pre, w1, kvn, wk, wv,
                                       (row2(q_norm[l]), wq, tabq, cosk, sink), tm=512)
    kc, vc, xbc_c, dt_c = _inproj(ctx, mod, bsz, pre, w1, kvn, wk, wv, None, tm=ctx.shape[1])
    attn = _attention(q, kx, kc, vx, vc)
    y = _ssd(xbc_x, xbc_c, dt_x, dt_c, ssd_conv_w[l], row2(ssd_conv_b[l]), dtb, alog, dexp)
    x1 = _outproj(attn, y, z, x, mod, wo[:N_ATTN_HEADS * V_HEAD_DIM], wo[N_ATTN_HEADS * V_HEAD_DIM:],
                  row2(ssd_norm[l]), row2(mix_post_norm[l]), tm=512)
    return _ffn(x1, mod, row2(ffn_pre_norm[l]), wg, wval, ffn_conv_w[l], row2(ffn_conv_b[l]), wd,
                row2(ffn_post_norm[l]), tf=256)
```

```python
import functools
import math

import jax
import jax.numpy as jnp
import numpy as np
from jax import lax
from jax.experimental import pallas as pl
from jax.experimental.pallas import tpu as pltpu

f32 = jnp.float32
bf16 = jnp.bfloat16

D_MODEL = 1024
GRID_W = 64
N_ATTN_HEADS = 16
QK_NOPE_DIM = 64
QK_ROPE_DIM = 32
V_HEAD_DIM = 64
Q_LORA_RANK = 384
KV_LORA_RANK = 256
ROPE_THETA = 10000.0
ATTN_SCALE = (QK_NOPE_DIM + QK_ROPE_DIM) ** -0.5
N_SSD_HEADS = 16
SSD_HEAD_DIM = 64
SSD_GROUPS = 2
HEADS_PER_GROUP = N_SSD_HEADS // SSD_GROUPS
SSD_STATE = 128
SSD_CONV = 5
SSD_CHUNK = 128
D_INNER = N_SSD_HEADS * SSD_HEAD_DIM
GN = SSD_GROUPS * SSD_STATE
XBC_WIDTH = D_INNER + 2 * GN
D_FF = 2816
FFN_CONV = 3
N_MOD = 6
EPS = 1e-6

LANES = 128
HEAD_PAD = 128
GROUP_W = HEADS_PER_GROUP * SSD_HEAD_DIM
VMEM_LIMIT = 56 * 1024 * 1024

_C_CQ = 0
_C_CKV = _C_CQ + Q_LORA_RANK
_C_KR = _C_CKV + KV_LORA_RANK
_C_Z = _C_KR + 2 * LANES
_C_XBC = _C_Z + D_INNER
_C_DT = _C_XBC + XBC_WIDTH
_C_END = _C_DT + SSD_GROUPS * LANES


def _cparams(sem, flags=None):
    return pltpu.CompilerParams(dimension_semantics=sem, vmem_limit_bytes=VMEM_LIMIT, flags=flags)


def _rms(v, w):
    return v * lax.rsqrt(jnp.mean(v * v, axis=-1, keepdims=True) + EPS) * w


def _sigmoid(v):
    return 1.0 / (1.0 + jnp.exp(-v))


def _dot(a, b):
    return jnp.dot(a, b, preferred_element_type=f32)


def _dot_nt(a, b):
    return lax.dot_general(a, b, (((1,), (1,)), ((), ())), preferred_element_type=f32)


def _mod_body(c_ref, w_ref, b_ref, o_ref):
    c = c_ref[...]
    s = (c * _sigmoid(c)).astype(bf16)
    o_ref[...] = _dot(s, w_ref[...]) + b_ref[...]


def _modulation(cvec, w_mod, b_mod):
    rows = cvec.shape[0]
    n = w_mod.shape[1]
    bn = D_MODEL
    return pl.pallas_call(
        _mod_body,
        out_shape=jax.ShapeDtypeStruct((rows, n), f32),
        grid=(n // bn,),
        in_specs=[pl.BlockSpec((rows, D_MODEL), lambda j: (0, 0)),
                  pl.BlockSpec((D_MODEL, bn), lambda j: (0, j)),
                  pl.BlockSpec((1, bn), lambda j: (0, j))],
        out_specs=pl.BlockSpec((rows, bn), lambda j: (0, j)),
        compiler_params=_cparams(("arbitrary",)),
        name="modulation",
    )(cvec, w_mod, b_mod)


def _inproj_body(*refs, is_ctx):
    if is_ctx:
        (x_ref, mod_ref, pre_ref, w1_ref, kvn_ref, wk_ref, wv_ref,
         k_ref, v_ref, xbc_ref, dt_ref) = refs
    else:
        (x_ref, mod_ref, pre_ref, w1_ref, kvn_ref, wk_ref, wv_ref, qn_ref, wq_ref,
         tabq_ref, cosk_ref, sink_ref,
         q_ref, k_ref, v_ref, z_ref, xbc_ref, dt_ref) = refs
    x = x_ref[0]
    shift = mod_ref[0, 0:1, :]
    scale = mod_ref[0, 1:2, :]
    h = (_rms(x, pre_ref[...]) * (1.0 + scale) + shift).astype(bf16)

    ckv = _dot(h, w1_ref[:, _C_CKV:_C_KR])
    ckvn = _rms(ckv, kvn_ref[...]).astype(bf16)
    v_ref[0] = _dot(ckvn, wv_ref[...]).astype(bf16)
    kres = _dot(ckvn, wk_ref[...])
    if is_ctx:
        kk = _dot(h, w1_ref[:, _C_KR:_C_KR + LANES])
    else:
        kr2 = _dot(h, w1_ref[:, _C_KR:_C_Z])
        kk = kr2[:, :LANES] * cosk_ref[...] + kr2[:, LANES:] * sink_ref[...]
    for hd in range(N_ATTN_HEADS):
        sl = slice(hd * HEAD_PAD, (hd + 1) * HEAD_PAD)
        k_ref[0, :, sl] = (kres[:, sl] + kk).astype(bf16)

    if not is_ctx:
        cq = _dot(h, w1_ref[:, _C_CQ:_C_CKV])
        cqn = _rms(cq, qn_ref[...]).astype(bf16)
        qres = _dot(cqn, wq_ref[...])
        tab = tabq_ref[...]
        for hd in range(N_ATTN_HEADS):
            sl = slice(hd * HEAD_PAD, (hd + 1) * HEAD_PAD)
            q_ref[0, :, sl] = (qres[:, sl] * tab).astype(bf16)
        z_ref[0] = _dot(h, w1_ref[:, _C_Z:_C_XBC]).astype(bf16)

    xbc_ref[0] = _dot(h, w1_ref[:, _C_XBC:_C_DT]).astype(bf16)
    dt_ref[0] = _dot(h, w1_ref[:, _C_DT:_C_END])


def _inproj(x, mod, mod_row, pre, w1, kvn, wk, wv, latent_args, tm):
    bsz, length, _ = x.shape
    is_ctx = latent_args is None
    nt = length // tm
    full = lambda a: pl.BlockSpec(a.shape, lambda b, i: (0,) * a.ndim)
    row = lambda w: pl.BlockSpec((1, tm, w), lambda b, i: (b, i, 0))
    tab = lambda a: pl.BlockSpec((tm, a.shape[1]), lambda b, i: (i, 0))
    if mod_row is None:
        mod_spec = pl.BlockSpec((1, N_MOD, D_MODEL), lambda b, i: (b, 0, 0))
    else:
        mod_spec = pl.BlockSpec((1, N_MOD, D_MODEL), lambda b, i: (mod_row, 0, 0))
    in_specs = [row(D_MODEL), mod_spec, full(pre), full(w1), full(kvn), full(wk), full(wv)]
    args = [x, mod, pre, w1, kvn, wk, wv]
    kw = N_ATTN_HEADS * HEAD_PAD
    vw = N_ATTN_HEADS * V_HEAD_DIM
    dtw = SSD_GROUPS * LANES
    sds = lambda w, dt: jax.ShapeDtypeStruct((bsz, length, w), dt)
    if is_ctx:
        out_shape = (sds(kw, bf16), sds(vw, bf16), sds(XBC_WIDTH, bf16), sds(dtw, f32))
        out_specs = (row(kw), row(vw), row(XBC_WIDTH), row(dtw))
    else:
        qn, wq, tabq, cosk, sink = latent_args
        in_specs += [full(qn), full(wq), tab(tabq), tab(cosk), tab(sink)]
        args += [qn, wq, tabq, cosk, sink]
        out_shape = (sds(kw, bf16), sds(kw, bf16), sds(vw, bf16), sds(D_INNER, bf16),
                     sds(XBC_WIDTH, bf16), sds(dtw, f32))
        out_specs = (row(kw), row(kw), row(vw), row(D_INNER), row(XBC_WIDTH), row(dtw))
    return pl.pallas_call(
        functools.partial(_inproj_body, is_ctx=is_ctx),
        out_shape=out_shape,
        grid=(bsz, nt),
        in_specs=in_specs,
        out_specs=out_specs,
        compiler_params=_cparams(("parallel", "parallel")),
        name="inproj_ctx" if is_ctx else "inproj_latent",
    )(*args)


_ATTN_TQ = 256
_ATTN_EXP_SPAN = 1.0


def _attn_body(q_ref, kx_ref, kc_ref, vx_ref, vc_ref, o_ref, s0, s1, m0, m1, p0, p1, vxa, vca):
    tq = _ATTN_TQ
    s = q_ref.shape[1]
    lc = kc_ref.shape[1]
    nt = s // tq
    lane_v = lax.broadcasted_iota(jnp.int32, (1, LANES), 1)
    own = [lane_v < V_HEAD_DIM, lane_v >= V_HEAD_DIM]
    for hh in range(2):
        vxa[hh] = jnp.where(own[hh], vx_ref[0].astype(f32), 1.0).astype(bf16)
        vca[hh] = jnp.where(own[hh], vc_ref[0].astype(f32), 1.0).astype(bf16)

    def rows_of(t):
        return slice(t * tq, (t + 1) * tq)

    kbw = lc
    nkb = (lc + s) // kbw

    def scores_items(t, slot):
        s_ref, m_ref = slot
        for hh in range(2):
            sl = slice(hh * HEAD_PAD, (hh + 1) * HEAD_PAD)
            run = {}
            for kb in range(nkb):
                def item(tok, hh=hh, sl=sl, run=run, kb=kb):
                    q = q_ref[0, rows_of(t), sl]
                    kblk = kc_ref[0, :, sl] if kb == 0 else kx_ref[0, (kb - 1) * kbw:kb * kbw, sl]
                    blk = _dot_nt(q, kblk)
                    s_ref[hh, :, kb * kbw:(kb + 1) * kbw] = blk
                    part = blk[:, 0:LANES]
                    for c in range(LANES, kbw, LANES):
                        part = jnp.maximum(part, blk[:, c:c + LANES])
                    run["m"] = part if kb == 0 else jnp.maximum(run["m"], part)
                    if kb == nkb - 1:
                        m_ref[hh] = run["m"]
                    return blk[tq - 16:tq, kbw - LANES:kbw]
                yield item

    def exps_items(slot, p_ref):
        s_ref, m_ref = slot
        strip = 16
        for hh in range(2):
            for r in range(0, tq, strip):
                def item(tok, hh=hh, r=r):
                    m = jnp.max(m_ref[hh, r:r + strip, :], axis=-1, keepdims=True)
                    if tok is not None:
                        tie = tok[:, 0:1]
                        m = jnp.where(tie == tie, m, tie)
                    for c in range(0, lc + s, LANES):
                        d = s_ref[hh, r:r + strip, c:c + LANES] - m
                        p_ref[hh, r:r + strip, c:c + LANES] = jnp.exp2(d.astype(bf16))
                yield item

    def outputs_items(p_ref, res):
        for hh in range(2):
            run = {}
            for kb in range(nkb):
                def item(tok, hh=hh, run=run, kb=kb):
                    vblk = vca[hh] if kb == 0 else vxa[hh, (kb - 1) * kbw:kb * kbw, :]
                    o = _dot(p_ref[hh, :, kb * kbw:(kb + 1) * kbw], vblk)
                    run["o"] = o if kb == 0 else run["o"] + o
                    if kb == nkb - 1:
                        o = run["o"]
                        den = o[:, V_HEAD_DIM:V_HEAD_DIM + 1] if hh == 0 else o[:, 0:1]
                        res.append(o / den)
                    return o[tq - 16:tq, :]
                yield item

    def run_interleaved(*streams, spans=None):
        items = []
        for si, stream in enumerate(streams):
            stream = list(stream)
            span = 1.0 if spans is None else spans[si]
            items += [(span * (i + 0.5) / len(stream), si, it) for i, it in enumerate(stream)]
        tok = None
        for _, _, it in sorted(items, key=lambda x: x[:2]):
            out = it(tok)
            tok = tok if out is None else out

    def store(t, res):
        o_ref[0, rows_of(t), :] = jnp.where(own[0], res[0], res[1]).astype(bf16)

    s_slot = ((s0, m0), (s1, m1))
    p_slot = (p0, p1)
    run_interleaved(scores_items(0, s_slot[0]))
    run_interleaved(exps_items(s_slot[0], p_slot[0]), scores_items(1, s_slot[1]))

    pl.delay(1)
    for t in range(2, nt):
        cur = t % 2
        res = []
        run_interleaved(outputs_items(p_slot[cur], res), exps_items(s_slot[1 - cur], p_slot[1 - cur]),
                        scores_items(t, s_slot[cur]), spans=(1.0, _ATTN_EXP_SPAN, 1.0))
        store(t - 2, res)
        pl.delay(1)
    last = (nt - 1) % 2
    res = []
    run_interleaved(outputs_items(p_slot[1 - last], res), exps_items(s_slot[last], p_slot[last]))
    store(nt - 2, res)
    res = []
    run_interleaved(outputs_items(p_slot[last], res))
    store(nt - 1, res)


def _attention(q, kx, kc, vx, vc):
    bsz, s, _ = q.shape
    lc = kc.shape[1]
    npair = N_ATTN_HEADS // 2
    tq = _ATTN_TQ
    assert s % (2 * tq) == 0 and s // tq >= 2
    seq = lambda length, w: pl.BlockSpec((1, length, w), lambda b, p: (b, 0, p))
    return pl.pallas_call(
        _attn_body,
        out_shape=jax.ShapeDtypeStruct((bsz, s, N_ATTN_HEADS * V_HEAD_DIM), bf16),
        grid=(bsz, npair),
        in_specs=[seq(s, 2 * HEAD_PAD), seq(s, 2 * HEAD_PAD), seq(lc, 2 * HEAD_PAD),
                  seq(s, 2 * V_HEAD_DIM), seq(lc, 2 * V_HEAD_DIM)],
        out_specs=seq(s, 2 * V_HEAD_DIM),
        scratch_shapes=[pltpu.VMEM((2, tq, lc + s), f32), pltpu.VMEM((2, tq, lc + s), f32),
                        pltpu.VMEM((2, tq, LANES), f32), pltpu.VMEM((2, tq, LANES), f32),
                        pltpu.VMEM((2, tq, lc + s), bf16), pltpu.VMEM((2, tq, lc + s), bf16),
                        pltpu.VMEM((2, s, 2 * V_HEAD_DIM), bf16), pltpu.VMEM((2, lc, 2 * V_HEAD_DIM), bf16)],
        compiler_params=_cparams(("parallel", "parallel")),
        name="attention",
    )(q, kx, kc, vx, vc)


_HALO = 16


def _conv_silu_pass(src_ref, w_ref, b_ref, dst_ref, length):
    q = SSD_CHUNK
    nc = length // q
    half = SSD_CONV // 2
    w = w_ref[...]
    bias = b_ref[...]

    def step(c, carry):
        t0 = pl.multiple_of(c * q, q)
        tp = pl.multiple_of(jnp.maximum(t0 - _HALO, 0), _HALO)
        tn = pl.multiple_of(jnp.minimum(t0 + q, length - _HALO), _HALO)
        prev = src_ref[0, pl.ds(tp, _HALO), :].astype(f32)
        nxt = src_ref[0, pl.ds(tn, _HALO), :].astype(f32)
        prev = jnp.where(c > 0, prev, 0.0)
        nxt = jnp.where(c < nc - 1, nxt, 0.0)
        win = jnp.concatenate([prev, src_ref[0, pl.ds(t0, q), :].astype(f32), nxt], axis=0)
        acc = bias + w[0:1, :] * win[_HALO - half:_HALO - half + q, :]
        for k in range(1, SSD_CONV):
            o = _HALO - half + k
            acc = acc + w[k:k + 1, :] * win[o:o + q, :]
        dst_ref[pl.ds(t0, q), :] = acc * _sigmoid(acc)
        return carry

    lax.fori_loop(0, nc, step, 0)


def _softplus(v):
    return jnp.maximum(v, 0.0) + jnp.log1p(jnp.exp(-jnp.abs(v)))


def _split3_dot(m_bf, a):
    a1 = a.astype(bf16)
    r1 = a - a1.astype(f32)
    a2 = r1.astype(bf16)
    a3 = (r1 - a2.astype(f32)).astype(bf16)
    return _dot(m_bf, a1) + _dot(m_bf, a2) + _dot(m_bf, a3)


def _split2(a):
    a1 = a.astype(bf16)
    a2 = (a - a1.astype(f32)).astype(bf16)
    return jnp.concatenate([a1, a2], axis=1)


def _ssd_chunk(xs, bm, cm, dt_raw, dt_bias, a_row, st_ref, sel_head, sel_half, d, out):
    want_y = out is not None
    q = SSD_CHUNK
    ri = lax.broadcasted_iota(jnp.int32, (q, q), 0)
    ci = lax.broadcasted_iota(jnp.int32, (q, q), 1)
    mask = (ci <= ri) if d == 0 else (ci >= ri)
    first_half = lax.broadcasted_iota(jnp.int32, (q, LANES), 1) < SSD_HEAD_DIM
    edge = q - 1 if d == 0 else 0

    dt = _softplus(dt_raw + dt_bias)
    a = dt * a_row
    tri = jnp.where(mask, 1.0, 0.0).astype(bf16)
    acum = _split3_dot(tri, a)
    yield
    a2 = _split2(acum)
    dtb = _dot(_split2(dt), sel_half)
    bt = bm.T.astype(bf16)
    if want_y:
        colb = _dot(a2, sel_head)
        at = acum.T
        cb = _dot_nt(cm.astype(bf16), bm.astype(bf16))
    else:
        col_half = _dot(a2, sel_half)
    yield
    ys = []
    for p in range(HEADS_PER_GROUP // 2):
        psl = slice(p * LANES, (p + 1) * LANES)
        xd = xs[:, psl] * dtb[:, psl]
        if want_y:
            lhs, cols = [], []
            for e in (2 * p, 2 * p + 1):
                j = d * HEADS_PER_GROUP + e
                col = colb[:, e * LANES:(e + 1) * LANES]
                seg = jnp.exp(jnp.where(mask, col - at[j:j + 1, :], -jnp.inf))
                lhs.append(jnp.concatenate([(cb * seg).astype(bf16),
                                            (cm * jnp.exp(col)).astype(bf16)], axis=1))
                cols.append(col)
            col2 = jnp.where(first_half, cols[0], cols[1])
        else:
            col2 = col_half[:, psl]
        tot = col2[edge:edge + 1, :]
        xdec = (xd * jnp.exp(tot - col2)).astype(bf16)
        yield
        st = st_ref[:, psl]
        if want_y:
            rhs = jnp.concatenate([xd.astype(bf16), st.astype(bf16)], axis=0)
            ys.append(jnp.where(first_half, _dot(lhs[0], rhs), _dot(lhs[1], rhs)))
        st_ref[:, psl] = jnp.exp(tot) * st + _dot(bt, xdec)
        yield
    if want_y:
        out.append(jnp.concatenate(ys, axis=1))


def _lockstep(gens):
    gens = list(gens)
    while gens:
        for g in list(gens):
            try:
                next(g)
            except StopIteration:
                gens.remove(g)


_SSD_UNROLL = 1


def _ssd_body(xs_x, b_x, c_x, xs_c, b_c, c_c, dt_x, dt_c, cw_xs, cw_b, cw_c, cb_xs, cb_b, cb_c,
              dtb_ref, alog_ref, dexp_ref, selh_ref, selq_ref, y_ref,
              xs_s, b_s, c_s, xsc_s, bc_s, cc_s, y_s, stf_ref, stb_ref):
    s = xs_x.shape[1]
    lc = xs_c.shape[1]
    q = SSD_CHUNK
    nc, ncc = s // q, lc // q
    _conv_silu_pass(xs_x, cw_xs, cb_xs, xs_s, s)
    _conv_silu_pass(b_x, cw_b, cb_b, b_s, s)
    _conv_silu_pass(c_x, cw_c, cb_c, c_s, s)
    _conv_silu_pass(xs_c, cw_xs, cb_xs, xsc_s, lc)
    _conv_silu_pass(b_c, cw_b, cb_b, bc_s, lc)
    _conv_silu_pass(c_c, cw_c, cb_c, cc_s, lc)

    dt_bias = dtb_ref[...]
    a_row = -jnp.exp(alog_ref[...])
    dexp = dexp_ref[...]
    st_refs = (stf_ref, stb_ref)
    for st_ref in st_refs:
        st_ref[...] = jnp.zeros(st_ref.shape, f32)

    def chunk_rows(i, n, d):
        c = i if d == 0 else n - 1 - i
        return pl.ds(pl.multiple_of(c * q, q), q)

    def scan_step(i, n, xs_r, b_r, c_r, dt_r, with_y, unroll):
        jobs = []
        for u in range(unroll):
            for d in range(2):
                rows = chunk_rows(i * unroll + u, n, d)
                out = [] if with_y else None
                gen = _ssd_chunk(xs_r[rows, :], b_r[rows, :], c_r[rows, :], dt_r[0, rows, :],
                                 dt_bias, a_row, st_refs[d], selh_ref[d], selq_ref[d], d, out)
                jobs.append((rows, out, gen))
        _lockstep(g for _, _, g in jobs)
        if with_y:
            for rows, out, _ in jobs:
                y_s[rows, :] = y_s[rows, :] + out[0]

    def ctx_step(i, carry):
        scan_step(i, ncc, xsc_s, bc_s, cc_s, dt_c, False, 1)
        return carry

    lax.fori_loop(0, ncc, ctx_step, 0)

    def skip_step(c, carry):
        rows = pl.ds(pl.multiple_of(c * q, q), q)
        y_s[rows, :] = dexp * xs_s[rows, :]
        return carry

    lax.fori_loop(0, nc, skip_step, 0)

    def lat_step(i, carry):
        scan_step(i, nc, xs_s, b_s, c_s, dt_x, True, _SSD_UNROLL)
        return carry

    assert nc % _SSD_UNROLL == 0
    lax.fori_loop(0, nc // _SSD_UNROLL, lat_step, 0)

    def out_step(c, carry):
        rows = pl.ds(pl.multiple_of(c * q, q), q)
        y_ref[0, rows, :] = y_s[rows, :].astype(bf16)
        return carry

    lax.fori_loop(0, nc, out_step, 0)


def _lane_select(width):
    sel = np.zeros((2, 2 * LANES, HEADS_PER_GROUP * width), np.float32)
    for d in range(2):
        for k in range(2):
            for e in range(HEADS_PER_GROUP):
                sel[d, k * LANES + d * HEADS_PER_GROUP + e, e * width:(e + 1) * width] = 1.0
    return jnp.asarray(sel, dtype=bf16)


def _ssd(xbc_x, xbc_c, dt_x, dt_c, conv_w, conv_b, dtb, alog, dexp):
    bsz, s, _ = xbc_x.shape
    lc = xbc_c.shape[1]
    selh = _lane_select(LANES)
    selq = _lane_select(SSD_HEAD_DIM)
    nxb = D_INNER // GROUP_W
    b_blk = D_INNER // SSD_STATE
    c_blk = (D_INNER + GN) // SSD_STATE
    seq = lambda length, w, blk: pl.BlockSpec((1, length, w), lambda b, g: (b, 0, blk(g)))
    par = lambda rows, w, blk: pl.BlockSpec((rows, w), lambda b, g: (0, blk(g)))
    xs_i = lambda g: g
    b_i = lambda g: b_blk + g
    c_i = lambda g: c_blk + g
    in_specs = [seq(s, GROUP_W, xs_i), seq(s, SSD_STATE, b_i), seq(s, SSD_STATE, c_i),
                seq(lc, GROUP_W, xs_i), seq(lc, SSD_STATE, b_i), seq(lc, SSD_STATE, c_i),
                seq(s, LANES, xs_i), seq(lc, LANES, xs_i),
                par(SSD_CONV, GROUP_W, xs_i), par(SSD_CONV, SSD_STATE, b_i), par(SSD_CONV, SSD_STATE, c_i),
                par(1, GROUP_W, xs_i), par(1, SSD_STATE, b_i), par(1, SSD_STATE, c_i),
                par(1, LANES, xs_i), par(1, LANES, xs_i), par(1, GROUP_W, xs_i),
                pl.BlockSpec(selh.shape, lambda b, g: (0, 0, 0)),
                pl.BlockSpec(selq.shape, lambda b, g: (0, 0, 0))]
    assert nxb == SSD_GROUPS
    scratch = [pltpu.VMEM((s, GROUP_W), f32), pltpu.VMEM((s, SSD_STATE), f32), pltpu.VMEM((s, SSD_STATE), f32),
               pltpu.VMEM((lc, GROUP_W), f32), pltpu.VMEM((lc, SSD_STATE), f32), pltpu.VMEM((lc, SSD_STATE), f32),
               pltpu.VMEM((s, GROUP_W), f32),
               pltpu.VMEM((SSD_STATE, GROUP_W), f32), pltpu.VMEM((SSD_STATE, GROUP_W), f32)]
    return pl.pallas_call(
        _ssd_body,
        out_shape=jax.ShapeDtypeStruct((bsz, s, D_INNER), bf16),
        grid=(bsz, SSD_GROUPS),
        in_specs=in_specs,
        out_specs=pl.BlockSpec((1, s, GROUP_W), lambda b, g: (b, 0, g)),
        scratch_shapes=scratch,
        compiler_params=_cparams(("parallel", "parallel")),
        name="ssd_scan",
    )(xbc_x, xbc_x, xbc_x, xbc_c, xbc_c, xbc_c, dt_x, dt_c, conv_w, conv_w, conv_w,
      conv_b, conv_b, conv_b, dtb, alog, dexp, selh, selq)


def _outproj_body(attn_ref, y_ref, z_ref, x_ref, mod_ref, wa_ref, ws_ref, sn_ref, pn_ref, o_ref):
    z = z_ref[0].astype(f32)
    g = y_ref[0].astype(f32) * (z * _sigmoid(z))
    gn = _rms(g, sn_ref[...]).astype(bf16)
    mix = _dot(attn_ref[0], wa_ref[...]) + _dot(gn, ws_ref[...])
    o_ref[0] = x_ref[0] + mod_ref[0, 2:3, :] * _rms(mix, pn_ref[...])


def _outproj(attn, y, z, x, mod, wa, ws, sn, pn, tm):
    bsz, s, _ = x.shape
    full = lambda a: pl.BlockSpec(a.shape, lambda b, i: (0,) * a.ndim)
    row = lambda w: pl.BlockSpec((1, tm, w), lambda b, i: (b, i, 0))
    return pl.pallas_call(
        _outproj_body,
        out_shape=jax.ShapeDtypeStruct((bsz, s, D_MODEL), f32),
        grid=(bsz, s // tm),
        in_specs=[row(attn.shape[2]), row(D_INNER), row(D_INNER), row(D_MODEL),
                  pl.BlockSpec((1, N_MOD, D_MODEL), lambda b, i: (b, 0, 0)),
                  full(wa), full(ws), full(sn), full(pn)],
        out_specs=row(D_MODEL),
        compiler_params=_cparams(("parallel", "parallel")),
        name="outproj",
    )(attn, y, z, x, mod, wa, ws, sn, pn)


_FFN_ROWS = 512
_FFN_PAD = 8


def _ffn_body(x_ref, mod_ref, pre_ref, wg_ref, wv_ref, cw_ref, cb_ref, wd_ref, pn_ref, o_ref, h_s, g_s):
    j = pl.program_id(1)
    nj = pl.num_programs(1)
    s = x_ref.shape[1]
    tf = wg_ref.shape[1]
    rb = _FFN_ROWS
    nrb = s // rb

    @pl.when(j == 0)
    def _():
        shift = mod_ref[0, 3:4, :]
        scale = mod_ref[0, 4:5, :]
        pre = pre_ref[...]

        def hstep(r, carry):
            rows = pl.ds(pl.multiple_of(r * rb, rb), rb)
            h_s[rows, :] = (_rms(x_ref[0, rows, :], pre) * (1.0 + scale) + shift).astype(bf16)
            o_ref[0, rows, :] = jnp.zeros((rb, D_MODEL), f32)
            return carry

        lax.fori_loop(0, nrb, hstep, 0)
        g_s[0:_FFN_PAD, :] = jnp.zeros((_FFN_PAD, tf), f32)
        g_s[_FFN_PAD + s:2 * _FFN_PAD + s, :] = jnp.zeros((_FFN_PAD, tf), f32)

    for r in range(nrb):
        r0 = r * rb
        g_s[r0 + _FFN_PAD:r0 + _FFN_PAD + rb, :] = _dot(h_s[r0:r0 + rb, :], wg_ref[...])

    cw = cw_ref[...]
    cb = cb_ref[...]
    for r in range(nrb):
        r0 = r * rb
        win = g_s[r0:r0 + rb + 2 * _FFN_PAD, :]
        gc = cb + cw[0:1, :] * win[_FFN_PAD - 1:_FFN_PAD - 1 + rb, :]
        gc = gc + cw[1:2, :] * win[_FFN_PAD:_FFN_PAD + rb, :]
        gc = gc + cw[2:3, :] * win[_FFN_PAD + 1:_FFN_PAD + 1 + rb, :]
        val = _dot(h_s[r0:r0 + rb, :], wv_ref[...])
        act = (0.5 * gc * (1.0 + lax.erf(gc * (1.0 / math.sqrt(2.0)))) * val).astype(bf16)
        o_ref[0, r0:r0 + rb, :] = o_ref[0, r0:r0 + rb, :] + _dot(act, wd_ref[...])

    @pl.when(j == nj - 1)
    def _():
        gate = mod_ref[0, 5:6, :]
        pn = pn_ref[...]

        def fstep(r, carry):
            rows = pl.ds(pl.multiple_of(r * rb, rb), rb)
            o_ref[0, rows, :] = x_ref[0, rows, :] + gate * _rms(o_ref[0, rows, :], pn)
            return carry

        lax.fori_loop(0, nrb, fstep, 0)


def _ffn(x1, mod, pre, wg, wv, cw, cb, wd, pn, tf):
    bsz, s, _ = x1.shape
    nj = D_FF // tf
    return pl.pallas_call(
        _ffn_body,
        out_shape=jax.ShapeDtypeStruct((bsz, s, D_MODEL), f32),
        grid=(bsz, nj),
        in_specs=[pl.BlockSpec((1, s, D_MODEL), lambda b, j: (b, 0, 0)),
                  pl.BlockSpec((1, N_MOD, D_MODEL), lambda b, j: (b, 0, 0)),
                  pl.BlockSpec((1, D_MODEL), lambda b, j: (0, 0)),
                  pl.BlockSpec((D_MODEL, tf), lambda b, j: (0, j)),
                  pl.BlockSpec((D_MODEL, tf), lambda b, j: (0, j)),
                  pl.BlockSpec((FFN_CONV, tf), lambda b, j: (0, j)),
                  pl.BlockSpec((1, tf), lambda b, j: (0, j)),
                  pl.BlockSpec((tf, D_MODEL), lambda b, j: (j, 0)),
                  pl.BlockSpec((1, D_MODEL), lambda b, j: (0, 0))],
        out_specs=pl.BlockSpec((1, s, D_MODEL), lambda b, j: (b, 0, 0)),
        scratch_shapes=[pltpu.VMEM((s, D_MODEL), bf16), pltpu.VMEM((s + 2 * _FFN_PAD, tf), f32)],
        compiler_params=_cparams(("parallel", "arbitrary")),
        name="convglu_ffn",
    )(x1, mod, pre, wg, wv, cw, cb, wd, pn)


def _rotate_half_axial(t):
    def rh(u):
        a, b = jnp.split(u, 2, axis=-1)
        return jnp.concatenate([-b, a], axis=-1)
    tr, tc = jnp.split(t, 2, axis=-1)
    return jnp.concatenate([rh(tr), rh(tc)], axis=-1)


def _rope_tables(seq_len):
    n_rows = seq_len // GRID_W
    row = jnp.repeat(jnp.arange(n_rows), GRID_W).astype(f32)
    col = jnp.tile(jnp.arange(GRID_W), n_rows).astype(f32)
    axis_dim = QK_ROPE_DIM // 2
    inv_freq = ROPE_THETA ** (-jnp.arange(0, axis_dim, 2, dtype=f32) / axis_dim)
    ang_r = row[:, None] * inv_freq
    ang_c = col[:, None] * inv_freq
    ang = jnp.concatenate([ang_r, ang_r, ang_c, ang_c], axis=-1)
    return jnp.cos(ang), jnp.sin(ang)


def _group_lanes(v):
    v = v.reshape(2, SSD_GROUPS, HEADS_PER_GROUP).transpose(1, 0, 2).reshape(SSD_GROUPS, 2 * HEADS_PER_GROUP)
    v = jnp.pad(v, ((0, 0), (0, LANES - 2 * HEADS_PER_GROUP)))
    return v.reshape(1, SSD_GROUPS * LANES)


def kernel(x, c, ctx, c_ctx, w_mod, b_mod, mix_pre_norm, mix_post_norm, w_in, q_norm, w_q_up, kv_norm, w_kv_up, ssd_conv_w, ssd_conv_b, ssd_a_log, ssd_dt_bias, ssd_d, ssd_norm, w_out, ffn_pre_norm, ffn_post_norm, w_up, ffn_conv_w, ffn_conv_b, w_down):
    bsz, seq, _ = x.shape
    assert w_mod.shape[0] == 1, "single-layer stack: the context stream is never updated"
    l = 0
    row2 = lambda v: v.reshape(1, -1)

    mod_rows = -(-(bsz + 1) // 8) * 8
    cvec = jnp.zeros((mod_rows, D_MODEL), f32).at[:bsz].set(c).at[bsz].set(c_ctx)

    wi = w_in[l]
    o_cq, o_ckv, o_kr, o_z, o_xbc, o_dt = np.cumsum([0, Q_LORA_RANK, KV_LORA_RANK, QK_ROPE_DIM, D_INNER, XBC_WIDTH])
    w_kr = wi[:, o_kr:o_z]
    w_kr_rot = _rotate_half_axial(w_kr)
    zpad = jnp.zeros((D_MODEL, QK_NOPE_DIM), f32)
    w_dt = wi[:, o_dt:].reshape(D_MODEL, 2, SSD_GROUPS, HEADS_PER_GROUP).transpose(0, 2, 1, 3)
    w_dt = jnp.pad(w_dt.reshape(D_MODEL, SSD_GROUPS, 2 * HEADS_PER_GROUP),
                   ((0, 0), (0, 0), (0, LANES - 2 * HEADS_PER_GROUP))).reshape(D_MODEL, SSD_GROUPS * LANES)
    w1 = jnp.concatenate([wi[:, o_cq:o_kr], zpad, w_kr, w_kr, zpad, w_kr_rot, w_kr_rot,
                          wi[:, o_z:o_dt], w_dt], axis=1).astype(bf16)
    assert w1.shape[1] == _C_END

    wq3 = w_q_up[l].reshape(Q_LORA_RANK, N_ATTN_HEADS, QK_NOPE_DIM + QK_ROPE_DIM)
    wq_rope = wq3[..., QK_NOPE_DIM:]
    wq = jnp.concatenate([wq3[..., :QK_NOPE_DIM], wq_rope, _rotate_half_axial(wq_rope)], axis=-1)
    wq = wq.reshape(Q_LORA_RANK, N_ATTN_HEADS * HEAD_PAD).astype(bf16)
    wkv3 = w_kv_up[l].reshape(KV_LORA_RANK, N_ATTN_HEADS, QK_NOPE_DIM + V_HEAD_DIM)
    wk = jnp.pad(wkv3[..., :QK_NOPE_DIM], ((0, 0), (0, 0), (0, HEAD_PAD - QK_NOPE_DIM)))
    wk = wk.reshape(KV_LORA_RANK, N_ATTN_HEADS * HEAD_PAD).astype(bf16)
    wv = wkv3[..., QK_NOPE_DIM:].reshape(KV_LORA_RANK, N_ATTN_HEADS * V_HEAD_DIM).astype(bf16)

    cos, sin = _rope_tables(seq)
    ones = jnp.ones((seq, QK_NOPE_DIM), f32)
    zeros = jnp.zeros((seq, QK_NOPE_DIM), f32)
    tabq = jnp.concatenate([ones, cos, sin], axis=1) * (ATTN_SCALE * math.log2(math.e))
    cosk = jnp.concatenate([zeros, cos, cos], axis=1)
    sink = jnp.concatenate([zeros, sin, sin], axis=1)

    dtb = _group_lanes(ssd_dt_bias[l])
    alog = _group_lanes(ssd_a_log[l])
    dexp = jnp.repeat(ssd_d[l], SSD_HEAD_DIM).reshape(1, D_INNER)

    wo = w_out[l].astype(bf16)
    wup = w_up[l].astype(bf16)
    wg, wval = wup[:, :D_FF], wup[:, D_FF:]
    wd = w_down[l].astype(bf16)

    mod = _modulation(cvec, w_mod[l].astype(bf16), row2(b_mod[l])).reshape(mod_rows, N_MOD, D_MODEL)
    pre = row2(mix_pre_norm[l])
    kvn = row2(kv_norm[l])
    q, kx, vx, z, xbc_x, dt_x = _inproj(x, mod, None, pre, w1, kvn, wk, wv,
                                       (row2(q_norm[l]), wq, tabq, cosk, sink), tm=512)
    kc, vc, xbc_c, dt_c = _inproj(ctx, mod, bsz, pre, w1, kvn, wk, wv, None, tm=ctx.shape[1])
    attn = _attention(q, kx, kc, vx, vc)
    y = _ssd(xbc_x, xbc_c, dt_x, dt_c, ssd_conv_w[l], row2(ssd_conv_b[l]), dtb, alog, dexp)
    x1 = _outproj(attn, y, z, x, mod, wo[:N_ATTN_HEADS * V_HEAD_DIM], wo[N_ATTN_HEADS * V_HEAD_DIM:],
                  row2(ssd_norm[l]), row2(mix_post_norm[l]), tm=512)
    return _ffn(x1, mod, row2(ffn_pre_norm[l]), wg, wval, ffn_conv_w[l], row2(ffn_conv_b[l]), wd,
                row2(ffn_post_norm[l]), tf=256)
```

```python
import functools
import math

import jax
import jax.numpy as jnp
import numpy as np
from jax import lax
from jax.experimental import pallas as pl
from jax.experimental.pallas import tpu as pltpu

f32 = jnp.float32
bf16 = jnp.bfloat16

D_MODEL = 1024
GRID_W = 64
N_ATTN_HEADS = 16
QK_NOPE_DIM = 64
QK_ROPE_DIM = 32
V_HEAD_DIM = 64
Q_LORA_RANK = 384
KV_LORA_RANK = 256
ROPE_THETA = 10000.0
ATTN_SCALE = (QK_NOPE_DIM + QK_ROPE_DIM) ** -0.5
N_SSD_HEADS = 16
SSD_HEAD_DIM = 64
SSD_GROUPS = 2
HEADS_PER_GROUP = N_SSD_HEADS // SSD_GROUPS
SSD_STATE = 128
SSD_CONV = 5
SSD_CHUNK = 128
D_INNER = N_SSD_HEADS * SSD_HEAD_DIM
GN = SSD_GROUPS * SSD_STATE
XBC_WIDTH = D_INNER + 2 * GN
D_FF = 2816
FFN_CONV = 3
N_MOD = 6
EPS = 1e-6

LANES = 128
HEAD_PAD = 128
GROUP_W = HEADS_PER_GROUP * SSD_HEAD_DIM
VMEM_LIMIT = 56 * 1024 * 1024

_C_CQ = 0
_C_CKV = _C_CQ + Q_LORA_RANK
_C_KR = _C_CKV + KV_LORA_RANK
_C_Z = _C_KR + 2 * LANES
_C_XBC = _C_Z + D_INNER
_C_DT = _C_XBC + XBC_WIDTH
_C_END = _C_DT + SSD_GROUPS * LANES


def _cparams(sem, flags=None):
    return pltpu.CompilerParams(dimension_semantics=sem, vmem_limit_bytes=VMEM_LIMIT, flags=flags)


def _rms(v, w):
    return v * lax.rsqrt(jnp.mean(v * v, axis=-1, keepdims=True) + EPS) * w


def _sigmoid(v):
    return 1.0 / (1.0 + jnp.exp(-v))


def _dot(a, b):
    return jnp.dot(a, b, preferred_element_type=f32)


def _dot_nt(a, b):
    return lax.dot_general(a, b, (((1,), (1,)), ((), ())), preferred_element_type=f32)


def _mod_body(c_ref, w_ref, b_ref, o_ref):
    c = c_ref[...]
    s = (c * _sigmoid(c)).astype(bf16)
    o_ref[...] = _dot(s, w_ref[...]) + b_ref[...]


def _modulation(cvec, w_mod, b_mod):
    rows = cvec.shape[0]
    n = w_mod.shape[1]
    bn = D_MODEL
    return pl.pallas_call(
        _mod_body,
        out_shape=jax.ShapeDtypeStruct((rows, n), f32),
        grid=(n // bn,),
        in_specs=[pl.BlockSpec((rows, D_MODEL), lambda j: (0, 0)),
                  pl.BlockSpec((D_MODEL, bn), lambda j: (0, j)),
                  pl.BlockSpec((1, bn), lambda j: (0, j))],
        out_specs=pl.BlockSpec((rows, bn), lambda j: (0, j)),
        compiler_params=_cparams(("arbitrary",)),
        name="modulation",
    )(cvec, w_mod, b_mod)


def _inproj_body(*refs, is_ctx):
    if is_ctx:
        (x_ref, mod_ref, pre_ref, w1_ref, kvn_ref, wk_ref, wv_ref,
         k_ref, v_ref, xbc_ref, dt_ref) = refs
    else:
        (x_ref, mod_ref, pre_ref, w1_ref, kvn_ref, wk_ref, wv_ref, qn_ref, wq_ref,
         tabq_ref, cosk_ref, sink_ref,
         q_ref, k_ref, v_ref, z_ref, xbc_ref, dt_ref) = refs
    x = x_ref[0]
    shift = mod_ref[0, 0:1, :]
    scale = mod_ref[0, 1:2, :]
    h = (_rms(x, pre_ref[...]) * (1.0 + scale) + shift).astype(bf16)

    ckv = _dot(h, w1_ref[:, _C_CKV:_C_KR])
    ckvn = _rms(ckv, kvn_ref[...]).astype(bf16)
    v_ref[0] = _dot(ckvn, wv_ref[...]).astype(bf16)
    kres = _dot(ckvn, wk_ref[...])
    if is_ctx:
        kk = _dot(h, w1_ref[:, _C_KR:_C_KR + LANES])
    else:
        kr2 = _dot(h, w1_ref[:, _C_KR:_C_Z])
        kk = kr2[:, :LANES] * cosk_ref[...] + kr2[:, LANES:] * sink_ref[...]
    for hd in range(N_ATTN_HEADS):
        sl = slice(hd * HEAD_PAD, (hd + 1) * HEAD_PAD)
        k_ref[0, :, sl] = (kres[:, sl] + kk).astype(bf16)

    if not is_ctx:
        cq = _dot(h, w1_ref[:, _C_CQ:_C_CKV])
        cqn = _rms(cq, qn_ref[...]).astype(bf16)
        qres = _dot(cqn, wq_ref[...])
        tab = tabq_ref[...]
        for hd in range(N_ATTN_HEADS):
            sl = slice(hd * HEAD_PAD, (hd + 1) * HEAD_PAD)
            q_ref[0, :, sl] = (qres[:, sl] * tab).astype(bf16)
        z_ref[0] = _dot(h, w1_ref[:, _C_Z:_C_XBC]).astype(bf16)

    xbc_ref[0] = _dot(h, w1_ref[:, _C_XBC:_C_DT]).astype(bf16)
    dt_ref[0] = _dot(h, w1_ref[:, _C_DT:_C_END])


def _inproj(x, mod, mod_row, pre, w1, kvn, wk, wv, latent_args, tm):
    bsz, length, _ = x.shape
    is_ctx = latent_args is None
    nt = length // tm
    full = lambda a: pl.BlockSpec(a.shape, lambda b, i: (0,) * a.ndim)
    row = lambda w: pl.BlockSpec((1, tm, w), lambda b, i: (b, i, 0))
    tab = lambda a: pl.BlockSpec((tm, a.shape[1]), lambda b, i: (i, 0))
    if mod_row is None:
        mod_spec = pl.BlockSpec((1, N_MOD, D_MODEL), lambda b, i: (b, 0, 0))
    else:
        mod_spec = pl.BlockSpec((1, N_MOD, D_MODEL), lambda b, i: (mod_row, 0, 0))
    in_specs = [row(D_MODEL), mod_spec, full(pre), full(w1), full(kvn), full(wk), full(wv)]
    args = [x, mod, pre, w1, kvn, wk, wv]
    kw = N_ATTN_HEADS * HEAD_PAD
    vw = N_ATTN_HEADS * V_HEAD_DIM
    dtw = SSD_GROUPS * LANES
    sds = lambda w, dt: jax.ShapeDtypeStruct((bsz, length, w), dt)
    if is_ctx:
        out_shape = (sds(kw, bf16), sds(vw, bf16), sds(XBC_WIDTH, bf16), sds(dtw, f32))
        out_specs = (row(kw), row(vw), row(XBC_WIDTH), row(dtw))
    else:
        qn, wq, tabq, cosk, sink = latent_args
        in_specs += [full(qn), full(wq), tab(tabq), tab(cosk), tab(sink)]
        args += [qn, wq, tabq, cosk, sink]
        out_shape = (sds(kw, bf16), sds(kw, bf16), sds(vw, bf16), sds(D_INNER, bf16),
                     sds(XBC_WIDTH, bf16), sds(dtw, f32))
        out_specs = (row(kw), row(kw), row(vw), row(D_INNER), row(XBC_WIDTH), row(dtw))
    return pl.pallas_call(
        functools.partial(_inproj_body, is_ctx=is_ctx),
        out_shape=out_shape,
        grid=(bsz, nt),
        in_specs=in_specs,
        out_specs=out_specs,
        compiler_params=_cparams(("parallel", "parallel")),
        name="inproj_ctx" if is_ctx else "inproj_latent",
    )(*args)


_ATTN_TQ = 256
_ATTN_EXP_SPAN = 1.0


def _attn_body(q_ref, kx_ref, kc_ref, vx_ref, vc_ref, o_ref, s0, s1, s2, m0, m1, m2, p0, p1, p2, vxa, vca):
    tq = _ATTN_TQ
    s = q_ref.shape[1]
    lc = kc_ref.shape[1]
    nt = s // tq
    lane_v = lax.broadcasted_iota(jnp.int32, (1, LANES), 1)
    own = [lane_v < V_HEAD_DIM, lane_v >= V_HEAD_DIM]
    for hh in range(2):
        vxa[hh] = jnp.where(own[hh], vx_ref[0].astype(f32), 1.0).astype(bf16)
        vca[hh] = jnp.where(own[hh], vc_ref[0].astype(f32), 1.0).astype(bf16)

    def rows_of(t):
        return slice(t * tq, (t + 1) * tq)

    kbw = lc
    nkb = (lc + s) // kbw

    def scores_items(t, slot):
        s_ref, m_ref = slot
        for hh in range(2):
            sl = slice(hh * HEAD_PAD, (hh + 1) * HEAD_PAD)
            run = {}
            for kb in range(nkb):
                def item(tok, hh=hh, sl=sl, run=run, kb=kb):
                    q = q_ref[0, rows_of(t), sl]
                    kblk = kc_ref[0, :, sl] if kb == 0 else kx_ref[0, (kb - 1) * kbw:kb * kbw, sl]
                    blk = _dot_nt(q, kblk)
                    s_ref[hh, :, kb * kbw:(kb + 1) * kbw] = blk
                    part = blk[:, 0:LANES]
                    for c in range(LANES, kbw, LANES):
                        part = jnp.maximum(part, blk[:, c:c + LANES])
                    run["m"] = part if kb == 0 else jnp.maximum(run["m"], part)
                    if kb == nkb - 1:
                        m_ref[hh] = run["m"]
                    return blk[tq - 16:tq, kbw - LANES:kbw]
                yield item

    def exps_items(slot, p_ref):
        s_ref, m_ref = slot
        strip = 16
        for hh in range(2):
            for r in range(0, tq, strip):
                def item(tok, hh=hh, r=r):
                    m = jnp.max(m_ref[hh, r:r + strip, :], axis=-1, keepdims=True)
                    if tok is not None:
                        tie = tok[:, 0:1]
                        m = jnp.where(tie == tie, m, tie)
                    for c in range(0, lc + s, LANES):
                        d = s_ref[hh, r:r + strip, c:c + LANES] - m
                        p_ref[hh, r:r + strip, c:c + LANES] = jnp.exp2(d.astype(bf16))
                yield item

    def outputs_items(p_ref, res):
        for hh in range(2):
            run = {}
            for kb in range(nkb):
                def item(tok, hh=hh, run=run, kb=kb):
                    vblk = vca[hh] if kb == 0 else vxa[hh, (kb - 1) * kbw:kb * kbw, :]
                    o = _dot(p_ref[hh, :, kb * kbw:(kb + 1) * kbw], vblk)
                    run["o"] = o if kb == 0 else run["o"] + o
                    if kb == nkb - 1:
                        o = run["o"]
                        den = o[:, V_HEAD_DIM:V_HEAD_DIM + 1] if hh == 0 else o[:, 0:1]
                        res.append(o / den)
                    return o[tq - 16:tq, :]
                yield item

    def run_interleaved(*streams, spans=None):
        items = []
        for si, stream in enumerate(streams):
            stream = list(stream)
            span = 1.0 if spans is None else spans[si]
            items += [(span * (i + 0.5) / len(stream), si, it) for i, it in enumerate(stream)]
        tok = None
        for _, _, it in sorted(items, key=lambda x: x[:2]):
            out = it(tok)
            tok = tok if out is None else out

    def store(t, res):
        o_ref[0, rows_of(t), :] = jnp.where(own[0], res[0], res[1]).astype(bf16)

    s_slot = ((s0, m0), (s1, m1), (s2, m2))
    p_slot = (p0, p1, p2)
    nslot = len(p_slot)
    run_interleaved(scores_items(0, s_slot[0]))
    run_interleaved(exps_items(s_slot[0], p_slot[0]), scores_items(1, s_slot[1]))

    for t in range(2, nt):
        res = []
        run_interleaved(outputs_items(p_slot[(t - 2) % nslot], res),
                        exps_items(s_slot[(t - 1) % nslot], p_slot[(t - 1) % nslot]),
                        scores_items(t, s_slot[t % nslot]), spans=(1.0, _ATTN_EXP_SPAN, 1.0))
        store(t - 2, res)
        if t % 2 == 0:
            pl.delay(1)
    res = []
    run_interleaved(outputs_items(p_slot[(nt - 2) % nslot], res),
                    exps_items(s_slot[(nt - 1) % nslot], p_slot[(nt - 1) % nslot]))
    store(nt - 2, res)
    res = []
    run_interleaved(outputs_items(p_slot[(nt - 1) % nslot], res))
    store(nt - 1, res)


def _attention(q, kx, kc, vx, vc):
    bsz, s, _ = q.shape
    lc = kc.shape[1]
    npair = N_ATTN_HEADS // 2
    tq = _ATTN_TQ
    assert s % (2 * tq) == 0 and s // tq >= 2
    seq = lambda length, w: pl.BlockSpec((1, length, w), lambda b, p: (b, 0, p))
    return pl.pallas_call(
        _attn_body,
        out_shape=jax.ShapeDtypeStruct((bsz, s, N_ATTN_HEADS * V_HEAD_DIM), bf16),
        grid=(bsz, npair),
        in_specs=[seq(s, 2 * HEAD_PAD), seq(s, 2 * HEAD_PAD), seq(lc, 2 * HEAD_PAD),
                  seq(s, 2 * V_HEAD_DIM), seq(lc, 2 * V_HEAD_DIM)],
        out_specs=seq(s, 2 * V_HEAD_DIM),
        scratch_shapes=[pltpu.VMEM((2, tq, lc + s), f32)] * 3 + [pltpu.VMEM((2, tq, LANES), f32)] * 3
                       + [pltpu.VMEM((2, tq, lc + s), bf16)] * 3 + [
                        pltpu.VMEM((2, s, 2 * V_HEAD_DIM), bf16), pltpu.VMEM((2, lc, 2 * V_HEAD_DIM), bf16)],
        compiler_params=_cparams(("parallel", "parallel")),
        name="attention",
    )(q, kx, kc, vx, vc)


_HALO = 16


def _conv_silu_pass(src_ref, w_ref, b_ref, dst_ref, length):
    q = SSD_CHUNK
    nc = length // q
    half = SSD_CONV // 2
    w = w_ref[...]
    bias = b_ref[...]

    def step(c, carry):
        t0 = pl.multiple_of(c * q, q)
        tp = pl.multiple_of(jnp.maximum(t0 - _HALO, 0), _HALO)
        tn = pl.multiple_of(jnp.minimum(t0 + q, length - _HALO), _HALO)
        prev = src_ref[0, pl.ds(tp, _HALO), :].astype(f32)
        nxt = src_ref[0, pl.ds(tn, _HALO), :].astype(f32)
        prev = jnp.where(c > 0, prev, 0.0)
        nxt = jnp.where(c < nc - 1, nxt, 0.0)
        win = jnp.concatenate([prev, src_ref[0, pl.ds(t0, q), :].astype(f32), nxt], axis=0)
        acc = bias + w[0:1, :] * win[_HALO - half:_HALO - half + q, :]
        for k in range(1, SSD_CONV):
            o = _HALO - half + k
            acc = acc + w[k:k + 1, :] * win[o:o + q, :]
        dst_ref[pl.ds(t0, q), :] = acc * _sigmoid(acc)
        return carry

    lax.fori_loop(0, nc, step, 0)


def _softplus(v):
    return jnp.maximum(v, 0.0) + jnp.log1p(jnp.exp(-jnp.abs(v)))


def _split3_dot(m_bf, a):
    a1 = a.astype(bf16)
    r1 = a - a1.astype(f32)
    a2 = r1.astype(bf16)
    a3 = (r1 - a2.astype(f32)).astype(bf16)
    return _dot(m_bf, a1) + _dot(m_bf, a2) + _dot(m_bf, a3)


def _split2(a):
    a1 = a.astype(bf16)
    a2 = (a - a1.astype(f32)).astype(bf16)
    return jnp.concatenate([a1, a2], axis=1)


def _ssd_chunk(xs, bm, cm, dt_raw, dt_bias, a_row, st_ref, sel_head, sel_half, d, out):
    want_y = out is not None
    q = SSD_CHUNK
    ri = lax.broadcasted_iota(jnp.int32, (q, q), 0)
    ci = lax.broadcasted_iota(jnp.int32, (q, q), 1)
    mask = (ci <= ri) if d == 0 else (ci >= ri)
    first_half = lax.broadcasted_iota(jnp.int32, (q, LANES), 1) < SSD_HEAD_DIM
    edge = q - 1 if d == 0 else 0

    dt = _softplus(dt_raw + dt_bias)
    a = dt * a_row
    tri = jnp.where(mask, 1.0, 0.0).astype(bf16)
    acum = _split3_dot(tri, a)
    yield
    a2 = _split2(acum)
    dtb = _dot(_split2(dt), sel_half)
    bt = bm.T.astype(bf16)
    if want_y:
        colb = _dot(a2, sel_head)
        at = acum.T
        cb = _dot_nt(cm.astype(bf16), bm.astype(bf16))
    else:
        col_half = _dot(a2, sel_half)
    yield
    ys = []
    for p in range(HEADS_PER_GROUP // 2):
        psl = slice(p * LANES, (p + 1) * LANES)
        xd = xs[:, psl] * dtb[:, psl]
        if want_y:
            lhs, cols = [], []
            for e in (2 * p, 2 * p + 1):
                j = d * HEADS_PER_GROUP + e
                col = colb[:, e * LANES:(e + 1) * LANES]
                seg = jnp.exp2(jnp.where(mask, col - at[j:j + 1, :], -jnp.inf))
                lhs.append(jnp.concatenate([(cb * seg).astype(bf16),
                                            (cm * jnp.exp2(col)).astype(bf16)], axis=1))
                cols.append(col)
            col2 = jnp.where(first_half, cols[0], cols[1])
        else:
            col2 = col_half[:, psl]
        tot = col2[edge:edge + 1, :]
        xdec = (xd * jnp.exp2(tot - col2)).astype(bf16)
        yield
        st = st_ref[:, psl]
        if want_y:
            rhs = jnp.concatenate([xd.astype(bf16), st.astype(bf16)], axis=0)
            ys.append(jnp.where(first_half, _dot(lhs[0], rhs), _dot(lhs[1], rhs)))
        st_ref[:, psl] = jnp.exp2(tot) * st + _dot(bt, xdec)
        yield
    if want_y:
        out.append(jnp.concatenate(ys, axis=1))


def _lockstep(gens):
    gens = list(gens)
    while gens:
        for g in list(gens):
            try:
                next(g)
            except StopIteration:
                gens.remove(g)


_SSD_UNROLL = 1


def _ssd_body(xs_x, b_x, c_x, xs_c, b_c, c_c, dt_x, dt_c, cw_xs, cw_b, cw_c, cb_xs, cb_b, cb_c,
              dtb_ref, alog_ref, dexp_ref, selh_ref, selq_ref, y_ref,
              xs_s, b_s, c_s, xsc_s, bc_s, cc_s, y_s, stf_ref, stb_ref):
    s = xs_x.shape[1]
    lc = xs_c.shape[1]
    q = SSD_CHUNK
    nc, ncc = s // q, lc // q
    _conv_silu_pass(xs_x, cw_xs, cb_xs, xs_s, s)
    _conv_silu_pass(b_x, cw_b, cb_b, b_s, s)
    _conv_silu_pass(c_x, cw_c, cb_c, c_s, s)
    _conv_silu_pass(xs_c, cw_xs, cb_xs, xsc_s, lc)
    _conv_silu_pass(b_c, cw_b, cb_b, bc_s, lc)
    _conv_silu_pass(c_c, cw_c, cb_c, cc_s, lc)

    dt_bias = dtb_ref[...]
    a_row = -jnp.exp(alog_ref[...]) * math.log2(math.e)
    dexp = dexp_ref[...]
    st_refs = (stf_ref, stb_ref)
    for st_ref in st_refs:
        st_ref[...] = jnp.zeros(st_ref.shape, f32)

    def chunk_rows(i, n, d):
        c = i if d == 0 else n - 1 - i
        return pl.ds(pl.multiple_of(c * q, q), q)

    def scan_step(i, n, xs_r, b_r, c_r, dt_r, with_y, unroll):
        jobs = []
        for u in range(unroll):
            for d in range(2):
                rows = chunk_rows(i * unroll + u, n, d)
                out = [] if with_y else None
                gen = _ssd_chunk(xs_r[rows, :], b_r[rows, :], c_r[rows, :], dt_r[0, rows, :],
                                 dt_bias, a_row, st_refs[d], selh_ref[d], selq_ref[d], d, out)
                jobs.append((rows, out, gen))
        _lockstep(g for _, _, g in jobs)
        if with_y:
            for rows, out, _ in jobs:
                y_s[rows, :] = y_s[rows, :] + out[0]

    def ctx_step(i, carry):
        scan_step(i, ncc, xsc_s, bc_s, cc_s, dt_c, False, 1)
        return carry

    lax.fori_loop(0, ncc, ctx_step, 0)

    def skip_step(c, carry):
        rows = pl.ds(pl.multiple_of(c * q, q), q)
        y_s[rows, :] = dexp * xs_s[rows, :]
        return carry

    lax.fori_loop(0, nc, skip_step, 0)

    def lat_step(i, carry):
        scan_step(i, nc, xs_s, b_s, c_s, dt_x, True, _SSD_UNROLL)
        return carry

    assert nc % _SSD_UNROLL == 0
    lax.fori_loop(0, nc // _SSD_UNROLL, lat_step, 0)

    def out_step(c, carry):
        rows = pl.ds(pl.multiple_of(c * q, q), q)
        y_ref[0, rows, :] = y_s[rows, :].astype(bf16)
        return carry

    lax.fori_loop(0, nc, out_step, 0)


def _lane_select(width):
    sel = np.zeros((2, 2 * LANES, HEADS_PER_GROUP * width), np.float32)
    for d in range(2):
        for k in range(2):
            for e in range(HEADS_PER_GROUP):
                sel[d, k * LANES + d * HEADS_PER_GROUP + e, e * width:(e + 1) * width] = 1.0
    return jnp.asarray(sel, dtype=bf16)


def _ssd(xbc_x, xbc_c, dt_x, dt_c, conv_w, conv_b, dtb, alog, dexp):
    bsz, s, _ = xbc_x.shape
    lc = xbc_c.shape[1]
    selh = _lane_select(LANES)
    selq = _lane_select(SSD_HEAD_DIM)
    nxb = D_INNER // GROUP_W
    b_blk = D_INNER // SSD_STATE
    c_blk = (D_INNER + GN) // SSD_STATE
    seq = lambda length, w, blk: pl.BlockSpec((1, length, w), lambda b, g: (b, 0, blk(g)))
    par = lambda rows, w, blk: pl.BlockSpec((rows, w), lambda b, g: (0, blk(g)))
    xs_i = lambda g: g
    b_i = lambda g: b_blk + g
    c_i = lambda g: c_blk + g
    in_specs = [seq(s, GROUP_W, xs_i), seq(s, SSD_STATE, b_i), seq(s, SSD_STATE, c_i),
                seq(lc, GROUP_W, xs_i), seq(lc, SSD_STATE, b_i), seq(lc, SSD_STATE, c_i),
                seq(s, LANES, xs_i), seq(lc, LANES, xs_i),
                par(SSD_CONV, GROUP_W, xs_i), par(SSD_CONV, SSD_STATE, b_i), par(SSD_CONV, SSD_STATE, c_i),
                par(1, GROUP_W, xs_i), par(1, SSD_STATE, b_i), par(1, SSD_STATE, c_i),
                par(1, LANES, xs_i), par(1, LANES, xs_i), par(1, GROUP_W, xs_i),
                pl.BlockSpec(selh.shape, lambda b, g: (0, 0, 0)),
                pl.BlockSpec(selq.shape, lambda b, g: (0, 0, 0))]
    assert nxb == SSD_GROUPS
    scratch = [pltpu.VMEM((s, GROUP_W), f32), pltpu.VMEM((s, SSD_STATE), f32), pltpu.VMEM((s, SSD_STATE), f32),
               pltpu.VMEM((lc, GROUP_W), f32), pltpu.VMEM((lc, SSD_STATE), f32), pltpu.VMEM((lc, SSD_STATE), f32),
               pltpu.VMEM((s, GROUP_W), f32),
               pltpu.VMEM((SSD_STATE, GROUP_W), f32), pltpu.VMEM((SSD_STATE, GROUP_W), f32)]
    return pl.pallas_call(
        _ssd_body,
        out_shape=jax.ShapeDtypeStruct((bsz, s, D_INNER), bf16),
        grid=(bsz, SSD_GROUPS),
        in_specs=in_specs,
        out_specs=pl.BlockSpec((1, s, GROUP_W), lambda b, g: (b, 0, g)),
        scratch_shapes=scratch,
        compiler_params=_cparams(("parallel", "parallel")),
        name="ssd_scan",
    )(xbc_x, xbc_x, xbc_x, xbc_c, xbc_c, xbc_c, dt_x, dt_c, conv_w, conv_w, conv_w,
      conv_b, conv_b, conv_b, dtb, alog, dexp, selh, selq)


def _outproj_body(attn_ref, y_ref, z_ref, x_ref, mod_ref, wa_ref, ws_ref, sn_ref, pn_ref, o_ref):
    z = z_ref[0].astype(f32)
    g = y_ref[0].astype(f32) * (z * _sigmoid(z))
    gn = _rms(g, sn_ref[...]).astype(bf16)
    mix = _dot(attn_ref[0], wa_ref[...]) + _dot(gn, ws_ref[...])
    o_ref[0] = x_ref[0] + mod_ref[0, 2:3, :] * _rms(mix, pn_ref[...])


def _outproj(attn, y, z, x, mod, wa, ws, sn, pn, tm):
    bsz, s, _ = x.shape
    full = lambda a: pl.BlockSpec(a.shape, lambda b, i: (0,) * a.ndim)
    row = lambda w: pl.BlockSpec((1, tm, w), lambda b, i: (b, i, 0))
    return pl.pallas_call(
        _outproj_body,
        out_shape=jax.ShapeDtypeStruct((bsz, s, D_MODEL), f32),
        grid=(bsz, s // tm),
        in_specs=[row(attn.shape[2]), row(D_INNER), row(D_INNER), row(D_MODEL),
                  pl.BlockSpec((1, N_MOD, D_MODEL), lambda b, i: (b, 0, 0)),
                  full(wa), full(ws), full(sn), full(pn)],
        out_specs=row(D_MODEL),
        compiler_params=_cparams(("parallel", "parallel")),
        name="outproj",
    )(attn, y, z, x, mod, wa, ws, sn, pn)


_FFN_ROWS = 512
_FFN_PAD = 8


def _ffn_body(x_ref, mod_ref, pre_ref, wg_ref, wv_ref, cw_ref, cb_ref, wd_ref, pn_ref, o_ref, h_s, g_s):
    j = pl.program_id(1)
    nj = pl.num_programs(1)
    s = x_ref.shape[1]
    tf = wg_ref.shape[1]
    rb = _FFN_ROWS
    nrb = s // rb

    @pl.when(j == 0)
    def _():
        shift = mod_ref[0, 3:4, :]
        scale = mod_ref[0, 4:5, :]
        pre = pre_ref[...]

        def hstep(r, carry):
            rows = pl.ds(pl.multiple_of(r * rb, rb), rb)
            h_s[rows, :] = (_rms(x_ref[0, rows, :], pre) * (1.0 + scale) + shift).astype(bf16)
            o_ref[0, rows, :] = jnp.zeros((rb, D_MODEL), f32)
            return carry

        lax.fori_loop(0, nrb, hstep, 0)
        g_s[0:_FFN_PAD, :] = jnp.zeros((_FFN_PAD, tf), f32)
        g_s[_FFN_PAD + s:2 * _FFN_PAD + s, :] = jnp.zeros((_FFN_PAD, tf), f32)

    for r in range(nrb):
        r0 = r * rb
        g_s[r0 + _FFN_PAD:r0 + _FFN_PAD + rb, :] = _dot(h_s[r0:r0 + rb, :], wg_ref[...])

    cw = cw_ref[...]
    cb = cb_ref[...]
    for r in range(nrb):
        r0 = r * rb
        win = g_s[r0:r0 + rb + 2 * _FFN_PAD, :]
        gc = cb + cw[0:1, :] * win[_FFN_PAD - 1:_FFN_PAD - 1 + rb, :]
        gc = gc + cw[1:2, :] * win[_FFN_PAD:_FFN_PAD + rb, :]
        gc = gc + cw[2:3, :] * win[_FFN_PAD + 1:_FFN_PAD + 1 + rb, :]
        val = _dot(h_s[r0:r0 + rb, :], wv_ref[...])
        act = (0.5 * gc * (1.0 + lax.erf(gc * (1.0 / math.sqrt(2.0)))) * val).astype(bf16)
        o_ref[0, r0:r0 + rb, :] = o_ref[0, r0:r0 + rb, :] + _dot(act, wd_ref[...])

    @pl.when(j == nj - 1)
    def _():
        gate = mod_ref[0, 5:6, :]
        pn = pn_ref[...]

        def fstep(r, carry):
            rows = pl.ds(pl.multiple_of(r * rb, rb), rb)
            o_ref[0, rows, :] = x_ref[0, rows, :] + gate * _rms(o_ref[0, rows, :], pn)
            return carry

        lax.fori_loop(0, nrb, fstep, 0)


def _ffn(x1, mod, pre, wg, wv, cw, cb, wd, pn, tf):
    bsz, s, _ = x1.shape
    nj = D_FF // tf
    return pl.pallas_call(
        _ffn_body,
        out_shape=jax.ShapeDtypeStruct((bsz, s, D_MODEL), f32),
        grid=(bsz, nj),
        in_specs=[pl.BlockSpec((1, s, D_MODEL), lambda b, j: (b, 0, 0)),
                  pl.BlockSpec((1, N_MOD, D_MODEL), lambda b, j: (b, 0, 0)),
                  pl.BlockSpec((1, D_MODEL), lambda b, j: (0, 0)),
                  pl.BlockSpec((D_MODEL, tf), lambda b, j: (0, j)),
                  pl.BlockSpec((D_MODEL, tf), lambda b, j: (0, j)),
                  pl.BlockSpec((FFN_CONV, tf), lambda b, j: (0, j)),
                  pl.BlockSpec((1, tf), lambda b, j: (0, j)),
                  pl.BlockSpec((tf, D_MODEL), lambda b, j: (j, 0)),
                  pl.BlockSpec((1, D_MODEL), lambda b, j: (0, 0))],
        out_specs=pl.BlockSpec((1, s, D_MODEL), lambda b, j: (b, 0, 0)),
        scratch_shapes=[pltpu.VMEM((s, D_MODEL), bf16), pltpu.VMEM((s + 2 * _FFN_PAD, tf), f32)],
        compiler_params=_cparams(("parallel", "arbitrary")),
        name="convglu_ffn",
    )(x1, mod, pre, wg, wv, cw, cb, wd, pn)


def _rotate_half_axial(t):
    def rh(u):
        a, b = jnp.split(u, 2, axis=-1)
        return jnp.concatenate([-b, a], axis=-1)
    tr, tc = jnp.split(t, 2, axis=-1)
    return jnp.concatenate([rh(tr), rh(tc)], axis=-1)


def _rope_tables(seq_len):
    n_rows = seq_len // GRID_W
    row = jnp.repeat(jnp.arange(n_rows), GRID_W).astype(f32)
    col = jnp.tile(jnp.arange(GRID_W), n_rows).astype(f32)
    axis_dim = QK_ROPE_DIM // 2
    inv_freq = ROPE_THETA ** (-jnp.arange(0, axis_dim, 2, dtype=f32) / axis_dim)
    ang_r = row[:, None] * inv_freq
    ang_c = col[:, None] * inv_freq
    ang = jnp.concatenate([ang_r, ang_r, ang_c, ang_c], axis=-1)
    return jnp.cos(ang), jnp.sin(ang)


def _group_lanes(v):
    v = v.reshape(2, SSD_GROUPS, HEADS_PER_GROUP).transpose(1, 0, 2).reshape(SSD_GROUPS, 2 * HEADS_PER_GROUP)
    v = jnp.pad(v, ((0, 0), (0, LANES - 2 * HEADS_PER_GROUP)))
    return v.reshape(1, SSD_GROUPS * LANES)


def kernel(x, c, ctx, c_ctx, w_mod, b_mod, mix_pre_norm, mix_post_norm, w_in, q_norm, w_q_up, kv_norm, w_kv_up, ssd_conv_w, ssd_conv_b, ssd_a_log, ssd_dt_bias, ssd_d, ssd_norm, w_out, ffn_pre_norm, ffn_post_norm, w_up, ffn_conv_w, ffn_conv_b, w_down):
    bsz, seq, _ = x.shape
    assert w_mod.shape[0] == 1, "single-layer stack: the context stream is never updated"
    l = 0
    row2 = lambda v: v.reshape(1, -1)

    mod_rows = -(-(bsz + 1) // 8) * 8
    cvec = jnp.zeros((mod_rows, D_MODEL), f32).at[:bsz].set(c).at[bsz].set(c_ctx)

    wi = w_in[l]
    o_cq, o_ckv, o_kr, o_z, o_xbc, o_dt = np.cumsum([0, Q_LORA_RANK, KV_LORA_RANK, QK_ROPE_DIM, D_INNER, XBC_WIDTH])
    w_kr = wi[:, o_kr:o_z]
    w_kr_rot = _rotate_half_axial(w_kr)
    zpad = jnp.zeros((D_MODEL, QK_NOPE_DIM), f32)
    w_dt = wi[:, o_dt:].reshape(D_MODEL, 2, SSD_GROUPS, HEADS_PER_GROUP).transpose(0, 2, 1, 3)
    w_dt = jnp.pad(w_dt.reshape(D_MODEL, SSD_GROUPS, 2 * HEADS_PER_GROUP),
                   ((0, 0), (0, 0), (0, LANES - 2 * HEADS_PER_GROUP))).reshape(D_MODEL, SSD_GROUPS * LANES)
    w1 = jnp.concatenate([wi[:, o_cq:o_kr], zpad, w_kr, w_kr, zpad, w_kr_rot, w_kr_rot,
                          wi[:, o_z:o_dt], w_dt], axis=1).astype(bf16)
    assert w1.shape[1] == _C_END

    wq3 = w_q_up[l].reshape(Q_LORA_RANK, N_ATTN_HEADS, QK_NOPE_DIM + QK_ROPE_DIM)
    wq_rope = wq3[..., QK_NOPE_DIM:]
    wq = jnp.concatenate([wq3[..., :QK_NOPE_DIM], wq_rope, _rotate_half_axial(wq_rope)], axis=-1)
    wq = wq.reshape(Q_LORA_RANK, N_ATTN_HEADS * HEAD_PAD).astype(bf16)
    wkv3 = w_kv_up[l].reshape(KV_LORA_RANK, N_ATTN_HEADS, QK_NOPE_DIM + V_HEAD_DIM)
    wk = jnp.pad(wkv3[..., :QK_NOPE_DIM], ((0, 0), (0, 0), (0, HEAD_PAD - QK_NOPE_DIM)))
    wk = wk.reshape(KV_LORA_RANK, N_ATTN_HEADS * HEAD_PAD).astype(bf16)
    wv = wkv3[..., QK_NOPE_DIM:].reshape(KV_LORA_RANK, N_ATTN_HEADS * V_HEAD_DIM).astype(bf16)

    cos, sin = _rope_tables(seq)
    ones = jnp.ones((seq, QK_NOPE_DIM), f32)
    zeros = jnp.zeros((seq, QK_NOPE_DIM), f32)
    tabq = jnp.concatenate([ones, cos, sin], axis=1) * (ATTN_SCALE * math.log2(math.e))
    cosk = jnp.concatenate([zeros, cos, cos], axis=1)
    sink = jnp.concatenate([zeros, sin, sin], axis=1)

    dtb = _group_lanes(ssd_dt_bias[l])
    alog = _group_lanes(ssd_a_log[l])
    dexp = jnp.repeat(ssd_d[l], SSD_HEAD_DIM).reshape(1, D_INNER)

    wo = w_out[l].astype(bf16)
    wup = w_up[l].astype(bf16)
    wg, wval = wup[:, :D_FF], wup[:, D_FF:]
    wd = w_down[l].astype(bf16)

    mod = _modulation(cvec, w_mod[l].astype(bf16), row2(b_mod[l])).reshape(mod_rows, N_MOD, D_MODEL)
    pre = row2(mix_pre_norm[l])
    kvn = row2(kv_norm[l])
    q, kx, vx, z, xbc_x, dt_x = _inproj(x, mod, None, pre, w1, kvn, wk, wv,
                                       (row2(q_norm[l]), wq, tabq, cosk, sink), tm=512)
    kc, vc, xbc_c, dt_c = _inproj(ctx, mod, bsz, pre, w1, kvn, wk, wv, None, tm=ctx.shape[1])
    attn = _attention(q, kx, kc, vx, vc)
    y = _ssd(xbc_x, xbc_c, dt_x, dt_c, ssd_conv_w[l], row2(ssd_conv_b[l]), dtb, alog, dexp)
    x1 = _outproj(attn, y, z, x, mod, wo[:N_ATTN_HEADS * V_HEAD_DIM], wo[N_ATTN_HEADS * V_HEAD_DIM:],
                  row2(ssd_norm[l]), row2(mix_post_norm[l]), tm=512)
    return _ffn(x1, mod, row2(ffn_pre_norm[l]), wg, wval, ffn_conv_w[l], row2(ffn_conv_b[l]), wd,
                row2(ffn_post_norm[l]), tf=256)
```

```python
import functools
import math

import jax
import jax.numpy as jnp
import numpy as np
from jax import lax
from jax.experimental import pallas as pl
from jax.experimental.pallas import tpu as pltpu

f32 = jnp.float32
bf16 = jnp.bfloat16

D_MODEL = 1024
GRID_W = 64
N_ATTN_HEADS = 16
QK_NOPE_DIM = 64
QK_ROPE_DIM = 32
V_HEAD_DIM = 64
Q_LORA_RANK = 384
KV_LORA_RANK = 256
ROPE_THETA = 10000.0
ATTN_SCALE = (QK_NOPE_DIM + QK_ROPE_DIM) ** -0.5
N_SSD_HEADS = 16
SSD_HEAD_DIM = 64
SSD_GROUPS = 2
HEADS_PER_GROUP = N_SSD_HEADS // SSD_GROUPS
SSD_STATE = 128
SSD_CONV = 5
SSD_CHUNK = 128
D_INNER = N_SSD_HEADS * SSD_HEAD_DIM
GN = SSD_GROUPS * SSD_STATE
XBC_WIDTH = D_INNER + 2 * GN
D_FF = 2816
FFN_CONV = 3
N_MOD = 6
EPS = 1e-6

LANES = 128
HEAD_PAD = 128
GROUP_W = HEADS_PER_GROUP * SSD_HEAD_DIM
VMEM_LIMIT = 56 * 1024 * 1024

_C_CQ = 0
_C_CKV = _C_CQ + Q_LORA_RANK
_C_KR = _C_CKV + KV_LORA_RANK
_C_Z = _C_KR + 2 * LANES
_C_XBC = _C_Z + D_INNER
_C_DT = _C_XBC + XBC_WIDTH
_C_END = _C_DT + SSD_GROUPS * LANES


def _cparams(sem, flags=None):
    return pltpu.CompilerParams(dimension_semantics=sem, vmem_limit_bytes=VMEM_LIMIT, flags=flags)


def _rms(v, w):
    return v * lax.rsqrt(jnp.mean(v * v, axis=-1, keepdims=True) + EPS) * w


def _sigmoid(v):
    return 1.0 / (1.0 + jnp.exp(-v))


def _dot(a, b):
    return jnp.dot(a, b, preferred_element_type=f32)


def _dot_nt(a, b):
    return lax.dot_general(a, b, (((1,), (1,)), ((), ())), preferred_element_type=f32)


def _mod_body(c_ref, w_ref, b_ref, o_ref):
    c = c_ref[...]
    s = (c * _sigmoid(c)).astype(bf16)
    o_ref[...] = _dot(s, w_ref[...]) + b_ref[...]


def _modulation(cvec, w_mod, b_mod):
    rows = cvec.shape[0]
    n = w_mod.shape[1]
    bn = D_MODEL
    return pl.pallas_call(
        _mod_body,
        out_shape=jax.ShapeDtypeStruct((rows, n), f32),
        grid=(n // bn,),
        in_specs=[pl.BlockSpec((rows, D_MODEL), lambda j: (0, 0)),
                  pl.BlockSpec((D_MODEL, bn), lambda j: (0, j)),
                  pl.BlockSpec((1, bn), lambda j: (0, j))],
        out_specs=pl.BlockSpec((rows, bn), lambda j: (0, j)),
        compiler_params=_cparams(("arbitrary",)),
        name="modulation",
    )(cvec, w_mod, b_mod)


def _inproj_body(*refs, is_ctx):
    if is_ctx:
        (x_ref, mod_ref, pre_ref, w1_ref, kvn_ref, wk_ref, wv_ref,
         k_ref, v_ref, xbc_ref, dt_ref) = refs
    else:
        (x_ref, mod_ref, pre_ref, w1_ref, kvn_ref, wk_ref, wv_ref, qn_ref, wq_ref,
         tabq_ref, cosk_ref, sink_ref,
         q_ref, k_ref, v_ref, z_ref, xbc_ref, dt_ref) = refs
    x = x_ref[0]
    shift = mod_ref[0, 0:1, :]
    scale = mod_ref[0, 1:2, :]
    h = (_rms(x, pre_ref[...]) * (1.0 + scale) + shift).astype(bf16)

    ckv = _dot(h, w1_ref[:, _C_CKV:_C_KR])
    ckvn = _rms(ckv, kvn_ref[...]).astype(bf16)
    v_ref[0] = _dot(ckvn, wv_ref[...]).astype(bf16)
    kres = _dot(ckvn, wk_ref[...])
    if is_ctx:
        kk = _dot(h, w1_ref[:, _C_KR:_C_KR + LANES])
    else:
        kr2 = _dot(h, w1_ref[:, _C_KR:_C_Z])
        kk = kr2[:, :LANES] * cosk_ref[...] + kr2[:, LANES:] * sink_ref[...]
    for hd in range(N_ATTN_HEADS):
        sl = slice(hd * HEAD_PAD, (hd + 1) * HEAD_PAD)
        k_ref[0, :, sl] = (kres[:, sl] + kk).astype(bf16)

    if not is_ctx:
        cq = _dot(h, w1_ref[:, _C_CQ:_C_CKV])
        cqn = _rms(cq, qn_ref[...]).astype(bf16)
        qres = _dot(cqn, wq_ref[...])
        tab = tabq_ref[...]
        for hd in range(N_ATTN_HEADS):
            sl = slice(hd * HEAD_PAD, (hd + 1) * HEAD_PAD)
            q_ref[0, :, sl] = (qres[:, sl] * tab).astype(bf16)
        z_ref[0] = _dot(h, w1_ref[:, _C_Z:_C_XBC]).astype(bf16)

    xbc_ref[0] = _dot(h, w1_ref[:, _C_XBC:_C_DT]).astype(bf16)
    dt_ref[0] = _dot(h, w1_ref[:, _C_DT:_C_END])


def _inproj(x, mod, mod_row, pre, w1, kvn, wk, wv, latent_args, tm):
    bsz, length, _ = x.shape
    is_ctx = latent_args is None
    nt = length // tm
    full = lambda a: pl.BlockSpec(a.shape, lambda b, i: (0,) * a.ndim)
    row = lambda w: pl.BlockSpec((1, tm, w), lambda b, i: (b, i, 0))
    tab = lambda a: pl.BlockSpec((tm, a.shape[1]), lambda b, i: (i, 0))
    if mod_row is None:
        mod_spec = pl.BlockSpec((1, N_MOD, D_MODEL), lambda b, i: (b, 0, 0))
    else:
        mod_spec = pl.BlockSpec((1, N_MOD, D_MODEL), lambda b, i: (mod_row, 0, 0))
    in_specs = [row(D_MODEL), mod_spec, full(pre), full(w1), full(kvn), full(wk), full(wv)]
    args = [x, mod, pre, w1, kvn, wk, wv]
    kw = N_ATTN_HEADS * HEAD_PAD
    vw = N_ATTN_HEADS * V_HEAD_DIM
    dtw = SSD_GROUPS * LANES
    sds = lambda w, dt: jax.ShapeDtypeStruct((bsz, length, w), dt)
    if is_ctx:
        out_shape = (sds(kw, bf16), sds(vw, bf16), sds(XBC_WIDTH, bf16), sds(dtw, f32))
        out_specs = (row(kw), row(vw), row(XBC_WIDTH), row(dtw))
    else:
        qn, wq, tabq, cosk, sink = latent_args
        in_specs += [full(qn), full(wq), tab(tabq), tab(cosk), tab(sink)]
        args += [qn, wq, tabq, cosk, sink]
        out_shape = (sds(kw, bf16), sds(kw, bf16), sds(vw, bf16), sds(D_INNER, bf16),
                     sds(XBC_WIDTH, bf16), sds(dtw, f32))
        out_specs = (row(kw), row(kw), row(vw), row(D_INNER), row(XBC_WIDTH), row(dtw))
    return pl.pallas_call(
        functools.partial(_inproj_body, is_ctx=is_ctx),
        out_shape=out_shape,
        grid=(bsz, nt),
        in_specs=in_specs,
        out_specs=out_specs,
        compiler_params=_cparams(("parallel", "parallel")),
        name="inproj_ctx" if is_ctx else "inproj_latent",
    )(*args)


_ATTN_TQ = 256
_ATTN_SLOTS = 4
_ATTN_PAIRS = 1
_ATTN_EXP_SPAN = 1.0


def _attn_body(q_ref, kx_ref, kc_ref, vx_ref, vc_ref, o_ref, *scratch):
    nslot = _ATTN_SLOTS
    s_bufs, m_bufs, p_bufs = scratch[:nslot], scratch[nslot:2 * nslot], scratch[2 * nslot:3 * nslot]
    vxa, vca = scratch[3 * nslot:]
    tq = _ATTN_TQ
    s = q_ref.shape[1]
    lc = kc_ref.shape[1]
    nt = s // tq
    lane_v = lax.broadcasted_iota(jnp.int32, (1, LANES), 1)
    own = [lane_v < V_HEAD_DIM, lane_v >= V_HEAD_DIM]
    npl = vx_ref.shape[2] // LANES
    for pi in range(npl):
        pl_lanes = slice(pi * LANES, (pi + 1) * LANES)
        for hh in range(2):
            vxa[2 * pi + hh] = jnp.where(own[hh], vx_ref[0, :, pl_lanes].astype(f32), 1.0).astype(bf16)
            vca[2 * pi + hh] = jnp.where(own[hh], vc_ref[0, :, pl_lanes].astype(f32), 1.0).astype(bf16)

    def rows_of(g):
        t = g % nt
        return slice(t * tq, (t + 1) * tq)

    kbw = lc
    nkb = (lc + s) // kbw

    def scores_items(t, slot):
        s_ref, m_ref = slot
        for hh in range(2):
            head = 2 * (t // nt) + hh
            sl = slice(head * HEAD_PAD, (head + 1) * HEAD_PAD)
            run = {}
            for kb in range(nkb):
                def item(tok, hh=hh, sl=sl, run=run, kb=kb):
                    q = q_ref[0, rows_of(t), sl]
                    kblk = kc_ref[0, :, sl] if kb == 0 else kx_ref[0, (kb - 1) * kbw:kb * kbw, sl]
                    blk = _dot_nt(q, kblk)
                    s_ref[hh, :, kb * kbw:(kb + 1) * kbw] = blk
                    part = blk[:, 0:LANES]
                    for c in range(LANES, kbw, LANES):
                        part = jnp.maximum(part, blk[:, c:c + LANES])
                    run["m"] = part if kb == 0 else jnp.maximum(run["m"], part)
                    if kb == nkb - 1:
                        m_ref[hh] = run["m"]
                    return blk[tq - 16:tq, kbw - LANES:kbw]
                yield item

    def exps_items(slot, p_ref):
        s_ref, m_ref = slot
        strip = 16
        for hh in range(2):
            for r in range(0, tq, strip):
                def item(tok, hh=hh, r=r):
                    m = jnp.max(m_ref[hh, r:r + strip, :], axis=-1, keepdims=True)
                    if tok is not None:
                        tie = tok[:, 0:1]
                        m = jnp.where(tie == tie, m, tie)
                    for c in range(0, lc + s, LANES):
                        d = s_ref[hh, r:r + strip, c:c + LANES] - m
                        p_ref[hh, r:r + strip, c:c + LANES] = jnp.exp2(d.astype(bf16))
                yield item

    def outputs_items(t, p_ref, res):
        for hh in range(2):
            run = {}
            head = 2 * (t // nt) + hh
            for kb in range(nkb):
                def item(tok, hh=hh, run=run, kb=kb, head=head):
                    vblk = vca[head] if kb == 0 else vxa[head, (kb - 1) * kbw:kb * kbw, :]
                    o = _dot(p_ref[hh, :, kb * kbw:(kb + 1) * kbw], vblk)
                    run["o"] = o if kb == 0 else run["o"] + o
                    if kb == nkb - 1:
                        o = run["o"]
                        den = o[:, V_HEAD_DIM:V_HEAD_DIM + 1] if hh == 0 else o[:, 0:1]
                        res.append(o / den)
                    return o[tq - 16:tq, :]
                yield item

    def run_interleaved(*streams, spans=None):
        items = []
        for si, stream in enumerate(streams):
            stream = list(stream)
            span = 1.0 if spans is None else spans[si]
            items += [(span * (i + 0.5) / len(stream), si, it) for i, it in enumerate(stream)]
        tok = None
        for _, _, it in sorted(items, key=lambda x: x[:2]):
            out = it(tok)
            tok = tok if out is None else out

    def store(t, res):
        pi = t // nt
        o_ref[0, rows_of(t), pi * LANES:(pi + 1) * LANES] = jnp.where(own[0], res[0], res[1]).astype(bf16)

    s_slot = tuple(zip(s_bufs, m_bufs))
    p_slot = tuple(p_bufs)
    run_interleaved(scores_items(0, s_slot[0]))
    run_interleaved(exps_items(s_slot[0], p_slot[0]), scores_items(1, s_slot[1]))

    ntot = npl * nt
    for t in range(2, ntot):
        res = []
        run_interleaved(outputs_items(t - 2, p_slot[(t - 2) % nslot], res),
                        exps_items(s_slot[(t - 1) % nslot], p_slot[(t - 1) % nslot]),
                        scores_items(t, s_slot[t % nslot]), spans=(1.0, _ATTN_EXP_SPAN, 1.0))
        store(t - 2, res)
        if t % (nslot - 1) == 0:
            pl.delay(1)
    res = []
    run_interleaved(outputs_items(ntot - 2, p_slot[(ntot - 2) % nslot], res),
                    exps_items(s_slot[(ntot - 1) % nslot], p_slot[(ntot - 1) % nslot]))
    store(ntot - 2, res)
    res = []
    run_interleaved(outputs_items(ntot - 1, p_slot[(ntot - 1) % nslot], res))
    store(ntot - 1, res)


def _attention(q, kx, kc, vx, vc):
    bsz, s, _ = q.shape
    lc = kc.shape[1]
    npl = _ATTN_PAIRS
    nheads = 2 * npl
    assert N_ATTN_HEADS % nheads == 0
    tq = _ATTN_TQ
    assert s % tq == 0 and s // tq >= 2
    seq = lambda length, w: pl.BlockSpec((1, length, w), lambda b, p: (b, 0, p))
    return pl.pallas_call(
        _attn_body,
        out_shape=jax.ShapeDtypeStruct((bsz, s, N_ATTN_HEADS * V_HEAD_DIM), bf16),
        grid=(bsz, N_ATTN_HEADS // nheads),
        in_specs=[seq(s, nheads * HEAD_PAD), seq(s, nheads * HEAD_PAD), seq(lc, nheads * HEAD_PAD),
                  seq(s, nheads * V_HEAD_DIM), seq(lc, nheads * V_HEAD_DIM)],
        out_specs=seq(s, nheads * V_HEAD_DIM),
        scratch_shapes=[pltpu.VMEM((2, tq, lc + s), f32)] * _ATTN_SLOTS
                       + [pltpu.VMEM((2, tq, LANES), f32)] * _ATTN_SLOTS
                       + [pltpu.VMEM((2, tq, lc + s), bf16)] * _ATTN_SLOTS + [
                        pltpu.VMEM((nheads, s, LANES), bf16), pltpu.VMEM((nheads, lc, LANES), bf16)],
        compiler_params=_cparams(("parallel", "parallel")),
        name="attention",
    )(q, kx, kc, vx, vc)


_HALO = 16


def _conv_silu_pass(src_ref, w_ref, b_ref, dst_ref, length):
    q = SSD_CHUNK
    nc = length // q
    half = SSD_CONV // 2
    w = w_ref[...]
    bias = b_ref[...]

    def step(c, carry):
        t0 = pl.multiple_of(c * q, q)
        tp = pl.multiple_of(jnp.maximum(t0 - _HALO, 0), _HALO)
        tn = pl.multiple_of(jnp.minimum(t0 + q, length - _HALO), _HALO)
        prev = src_ref[0, pl.ds(tp, _HALO), :].astype(f32)
        nxt = src_ref[0, pl.ds(tn, _HALO), :].astype(f32)
        prev = jnp.where(c > 0, prev, 0.0)
        nxt = jnp.where(c < nc - 1, nxt, 0.0)
        win = jnp.concatenate([prev, src_ref[0, pl.ds(t0, q), :].astype(f32), nxt], axis=0)
        sub = 8
        win = win[_HALO - sub:_HALO + q + sub, :]
        n = q + 2 * sub
        acc = bias
        for k in range(SSD_CONV):
            shift = (half - k) % n
            tap = win if shift == 0 else pltpu.roll(win, shift, 0)
            acc = acc + w[k:k + 1, :] * tap[sub:sub + q, :]
        dst_ref[pl.ds(t0, q), :] = acc * _sigmoid(acc)
        return carry

    lax.fori_loop(0, nc, step, 0)


def _softplus(v):
    return jnp.maximum(v, 0.0) + jnp.log1p(jnp.exp(-jnp.abs(v)))


def _split3_dot(m_bf, a):
    a1 = a.astype(bf16)
    r1 = a - a1.astype(f32)
    a2 = r1.astype(bf16)
    a3 = (r1 - a2.astype(f32)).astype(bf16)
    return _dot(m_bf, a1) + _dot(m_bf, a2) + _dot(m_bf, a3)


def _split2(a):
    a1 = a.astype(bf16)
    a2 = (a - a1.astype(f32)).astype(bf16)
    return jnp.concatenate([a1, a2], axis=1)


def _ssd_chunk(xs, bm, cm, dt_raw, dt_bias, a_row, st_ref, sel_head, sel_half, d, out):
    want_y = out is not None
    q = SSD_CHUNK
    ri = lax.broadcasted_iota(jnp.int32, (q, q), 0)
    ci = lax.broadcasted_iota(jnp.int32, (q, q), 1)
    mask = (ci <= ri) if d == 0 else (ci >= ri)
    first_half = lax.broadcasted_iota(jnp.int32, (q, LANES), 1) < SSD_HEAD_DIM
    edge = q - 1 if d == 0 else 0

    dt = _softplus(dt_raw + dt_bias)
    a = dt * a_row
    tri = jnp.where(mask, 1.0, 0.0).astype(bf16)
    acum = _split3_dot(tri, a)
    yield
    a2 = _split2(acum)
    dtb = _dot(_split2(dt), sel_half)
    bt = bm.T.astype(bf16)
    if want_y:
        colb = _dot(a2, sel_head)
        at = acum.T
        cb = _dot_nt(cm.astype(bf16), bm.astype(bf16))
    else:
        col_half = _dot(a2, sel_half)
    yield
    ys = []
    for p in range(HEADS_PER_GROUP // 2):
        psl = slice(p * LANES, (p + 1) * LANES)
        xd = xs[:, psl] * dtb[:, psl]
        if want_y:
            lhs, cols = [], []
            for e in (2 * p, 2 * p + 1):
                j = d * HEADS_PER_GROUP + e
                col = colb[:, e * LANES:(e + 1) * LANES]
                seg = jnp.exp2(jnp.where(mask, col - at[j:j + 1, :], -jnp.inf))
                lhs.append(jnp.concatenate([(cb * seg).astype(bf16),
                                            (cm * jnp.exp2(col)).astype(bf16)], axis=1))
                cols.append(col)
            col2 = jnp.where(first_half, cols[0], cols[1])
        else:
            col2 = col_half[:, psl]
        tot = col2[edge:edge + 1, :]
        xdec = (xd * jnp.exp2(tot - col2)).astype(bf16)
        yield
        st = st_ref[:, psl]
        if want_y:
            rhs = jnp.concatenate([xd.astype(bf16), st.astype(bf16)], axis=0)
            ys.append(jnp.where(first_half, _dot(lhs[0], rhs), _dot(lhs[1], rhs)))
        st_ref[:, psl] = jnp.exp2(tot) * st + _dot(bt, xdec)
        yield
    if want_y:
        out.append(jnp.concatenate(ys, axis=1))


def _lockstep(gens):
    gens = list(gens)
    while gens:
        for g in list(gens):
            try:
                next(g)
            except StopIteration:
                gens.remove(g)


_SSD_UNROLL = 1


def _ssd_body(xs_x, b_x, c_x, xs_c, b_c, c_c, dt_x, dt_c, cw_xs, cw_b, cw_c, cb_xs, cb_b, cb_c,
              dtb_ref, alog_ref, dexp_ref, selh_ref, selq_ref, y_ref,
              xs_s, b_s, c_s, xsc_s, bc_s, cc_s, y_s, stf_ref, stb_ref):
    s = xs_x.shape[1]
    lc = xs_c.shape[1]
    q = SSD_CHUNK
    nc, ncc = s // q, lc // q
    _conv_silu_pass(xs_x, cw_xs, cb_xs, xs_s, s)
    _conv_silu_pass(b_x, cw_b, cb_b, b_s, s)
    _conv_silu_pass(c_x, cw_c, cb_c, c_s, s)
    _conv_silu_pass(xs_c, cw_xs, cb_xs, xsc_s, lc)
    _conv_silu_pass(b_c, cw_b, cb_b, bc_s, lc)
    _conv_silu_pass(c_c, cw_c, cb_c, cc_s, lc)

    dt_bias = dtb_ref[...]
    a_row = -jnp.exp(alog_ref[...]) * math.log2(math.e)
    dexp = dexp_ref[...]
    st_refs = (stf_ref, stb_ref)
    for st_ref in st_refs:
        st_ref[...] = jnp.zeros(st_ref.shape, f32)

    def chunk_rows(i, n, d):
        c = i if d == 0 else n - 1 - i
        return pl.ds(pl.multiple_of(c * q, q), q)

    def scan_step(i, n, xs_r, b_r, c_r, dt_r, with_y, unroll):
        jobs = []
        for u in range(unroll):
            for d in range(2):
                rows = chunk_rows(i * unroll + u, n, d)
                out = [] if with_y else None
                gen = _ssd_chunk(xs_r[rows, :], b_r[rows, :], c_r[rows, :], dt_r[0, rows, :],
                                 dt_bias, a_row, st_refs[d], selh_ref[d], selq_ref[d], d, out)
                jobs.append((rows, out, gen))
        _lockstep(g for _, _, g in jobs)
        if with_y:
            for rows, out, _ in jobs:
                y_s[rows, :] = y_s[rows, :] + out[0]

    def ctx_step(i, carry):
        scan_step(i, ncc, xsc_s, bc_s, cc_s, dt_c, False, 1)
        return carry

    lax.fori_loop(0, ncc, ctx_step, 0)

    def skip_step(c, carry):
        rows = pl.ds(pl.multiple_of(c * q, q), q)
        y_s[rows, :] = dexp * xs_s[rows, :]
        return carry

    lax.fori_loop(0, nc, skip_step, 0)

    def lat_step(i, carry):
        scan_step(i, nc, xs_s, b_s, c_s, dt_x, True, _SSD_UNROLL)
        return carry

    assert nc % _SSD_UNROLL == 0
    lax.fori_loop(0, nc // _SSD_UNROLL, lat_step, 0)

    def out_step(c, carry):
        rows = pl.ds(pl.multiple_of(c * q, q), q)
        y_ref[0, rows, :] = y_s[rows, :].astype(bf16)
        return carry

    lax.fori_loop(0, nc, out_step, 0)


def _lane_select(width):
    sel = np.zeros((2, 2 * LANES, HEADS_PER_GROUP * width), np.float32)
    for d in range(2):
        for k in range(2):
            for e in range(HEADS_PER_GROUP):
                sel[d, k * LANES + d * HEADS_PER_GROUP + e, e * width:(e + 1) * width] = 1.0
    return jnp.asarray(sel, dtype=bf16)


def _ssd(xbc_x, xbc_c, dt_x, dt_c, conv_w, conv_b, dtb, alog, dexp):
    bsz, s, _ = xbc_x.shape
    lc = xbc_c.shape[1]
    selh = _lane_select(LANES)
    selq = _lane_select(SSD_HEAD_DIM)
    nxb = D_INNER // GROUP_W
    b_blk = D_INNER // SSD_STATE
    c_blk = (D_INNER + GN) // SSD_STATE
    seq = lambda length, w, blk: pl.BlockSpec((1, length, w), lambda b, g: (b, 0, blk(g)))
    par = lambda rows, w, blk: pl.BlockSpec((rows, w), lambda b, g: (0, blk(g)))
    xs_i = lambda g: g
    b_i = lambda g: b_blk + g
    c_i = lambda g: c_blk + g
    in_specs = [seq(s, GROUP_W, xs_i), seq(s, SSD_STATE, b_i), seq(s, SSD_STATE, c_i),
                seq(lc, GROUP_W, xs_i), seq(lc, SSD_STATE, b_i), seq(lc, SSD_STATE, c_i),
                seq(s, LANES, xs_i), seq(lc, LANES, xs_i),
                par(SSD_CONV, GROUP_W, xs_i), par(SSD_CONV, SSD_STATE, b_i), par(SSD_CONV, SSD_STATE, c_i),
                par(1, GROUP_W, xs_i), par(1, SSD_STATE, b_i), par(1, SSD_STATE, c_i),
                par(1, LANES, xs_i), par(1, LANES, xs_i), par(1, GROUP_W, xs_i),
                pl.BlockSpec(selh.shape, lambda b, g: (0, 0, 0)),
                pl.BlockSpec(selq.shape, lambda b, g: (0, 0, 0))]
    assert nxb == SSD_GROUPS
    scratch = [pltpu.VMEM((s, GROUP_W), f32), pltpu.VMEM((s, SSD_STATE), f32), pltpu.VMEM((s, SSD_STATE), f32),
               pltpu.VMEM((lc, GROUP_W), f32), pltpu.VMEM((lc, SSD_STATE), f32), pltpu.VMEM((lc, SSD_STATE), f32),
               pltpu.VMEM((s, GROUP_W), f32),
               pltpu.VMEM((SSD_STATE, GROUP_W), f32), pltpu.VMEM((SSD_STATE, GROUP_W), f32)]
    return pl.pallas_call(
        _ssd_body,
        out_shape=jax.ShapeDtypeStruct((bsz, s, D_INNER), bf16),
        grid=(bsz, SSD_GROUPS),
        in_specs=in_specs,
        out_specs=pl.BlockSpec((1, s, GROUP_W), lambda b, g: (b, 0, g)),
        scratch_shapes=scratch,
        compiler_params=_cparams(("parallel", "parallel")),
        name="ssd_scan",
    )(xbc_x, xbc_x, xbc_x, xbc_c, xbc_c, xbc_c, dt_x, dt_c, conv_w, conv_w, conv_w,
      conv_b, conv_b, conv_b, dtb, alog, dexp, selh, selq)


def _outproj_body(attn_ref, y_ref, z_ref, x_ref, mod_ref, wa_ref, ws_ref, sn_ref, pn_ref, o_ref):
    z = z_ref[0].astype(f32)
    g = y_ref[0].astype(f32) * (z * _sigmoid(z))
    gn = _rms(g, sn_ref[...]).astype(bf16)
    mix = _dot(attn_ref[0], wa_ref[...]) + _dot(gn, ws_ref[...])
    o_ref[0] = x_ref[0] + mod_ref[0, 2:3, :] * _rms(mix, pn_ref[...])


def _outproj(attn, y, z, x, mod, wa, ws, sn, pn, tm):
    bsz, s, _ = x.shape
    full = lambda a: pl.BlockSpec(a.shape, lambda b, i: (0,) * a.ndim)
    row = lambda w: pl.BlockSpec((1, tm, w), lambda b, i: (b, i, 0))
    return pl.pallas_call(
        _outproj_body,
        out_shape=jax.ShapeDtypeStruct((bsz, s, D_MODEL), f32),
        grid=(bsz, s // tm),
        in_specs=[row(attn.shape[2]), row(D_INNER), row(D_INNER), row(D_MODEL),
                  pl.BlockSpec((1, N_MOD, D_MODEL), lambda b, i: (b, 0, 0)),
                  full(wa), full(ws), full(sn), full(pn)],
        out_specs=row(D_MODEL),
        compiler_params=_cparams(("parallel", "parallel")),
        name="outproj",
    )(attn, y, z, x, mod, wa, ws, sn, pn)


_FFN_ROWS = 512
_FFN_PAD = 8


def _ffn_body(x_ref, mod_ref, pre_ref, wg_ref, wv_ref, cw_ref, cb_ref, wd_ref, pn_ref, o_ref, h_s, g_s):
    j = pl.program_id(1)
    nj = pl.num_programs(1)
    s = x_ref.shape[1]
    tf = wg_ref.shape[1]
    rb = _FFN_ROWS
    nrb = s // rb

    @pl.when(j == 0)
    def _():
        shift = mod_ref[0, 3:4, :]
        scale = mod_ref[0, 4:5, :]
        pre = pre_ref[...]

        def hstep(r, carry):
            rows = pl.ds(pl.multiple_of(r * rb, rb), rb)
            h_s[rows, :] = (_rms(x_ref[0, rows, :], pre) * (1.0 + scale) + shift).astype(bf16)
            o_ref[0, rows, :] = jnp.zeros((rb, D_MODEL), f32)
            return carry

        lax.fori_loop(0, nrb, hstep, 0)
        g_s[0:_FFN_PAD, :] = jnp.zeros((_FFN_PAD, tf), f32)
        g_s[_FFN_PAD + s:2 * _FFN_PAD + s, :] = jnp.zeros((_FFN_PAD, tf), f32)

    for r in range(nrb):
        r0 = r * rb
        g_s[r0 + _FFN_PAD:r0 + _FFN_PAD + rb, :] = _dot(h_s[r0:r0 + rb, :], wg_ref[...])

    cw = cw_ref[...]
    cb = cb_ref[...]
    for r in range(nrb):
        r0 = r * rb
        win = g_s[r0:r0 + rb + 2 * _FFN_PAD, :]
        gc = cb + cw[0:1, :] * win[_FFN_PAD - 1:_FFN_PAD - 1 + rb, :]
        gc = gc + cw[1:2, :] * win[_FFN_PAD:_FFN_PAD + rb, :]
        gc = gc + cw[2:3, :] * win[_FFN_PAD + 1:_FFN_PAD + 1 + rb, :]
        val = _dot(h_s[r0:r0 + rb, :], wv_ref[...])
        act = (0.5 * gc * (1.0 + lax.erf(gc * (1.0 / math.sqrt(2.0)))) * val).astype(bf16)
        o_ref[0, r0:r0 + rb, :] = o_ref[0, r0:r0 + rb, :] + _dot(act, wd_ref[...])

    @pl.when(j == nj - 1)
    def _():
        gate = mod_ref[0, 5:6, :]
        pn = pn_ref[...]

        def fstep(r, carry):
            rows = pl.ds(pl.multiple_of(r * rb, rb), rb)
            o_ref[0, rows, :] = x_ref[0, rows, :] + gate * _rms(o_ref[0, rows, :], pn)
            return carry

        lax.fori_loop(0, nrb, fstep, 0)


def _ffn(x1, mod, pre, wg, wv, cw, cb, wd, pn, tf):
    bsz, s, _ = x1.shape
    nj = D_FF // tf
    return pl.pallas_call(
        _ffn_body,
        out_shape=jax.ShapeDtypeStruct((bsz, s, D_MODEL), f32),
        grid=(bsz, nj),
        in_specs=[pl.BlockSpec((1, s, D_MODEL), lambda b, j: (b, 0, 0)),
                  pl.BlockSpec((1, N_MOD, D_MODEL), lambda b, j: (b, 0, 0)),
                  pl.BlockSpec((1, D_MODEL), lambda b, j: (0, 0)),
                  pl.BlockSpec((D_MODEL, tf), lambda b, j: (0, j)),
                  pl.BlockSpec((D_MODEL, tf), lambda b, j: (0, j)),
                  pl.BlockSpec((FFN_CONV, tf), lambda b, j: (0, j)),
                  pl.BlockSpec((1, tf), lambda b, j: (0, j)),
                  pl.BlockSpec((tf, D_MODEL), lambda b, j: (j, 0)),
                  pl.BlockSpec((1, D_MODEL), lambda b, j: (0, 0))],
        out_specs=pl.BlockSpec((1, s, D_MODEL), lambda b, j: (b, 0, 0)),
        scratch_shapes=[pltpu.VMEM((s, D_MODEL), bf16), pltpu.VMEM((s + 2 * _FFN_PAD, tf), f32)],
        compiler_params=_cparams(("parallel", "arbitrary")),
        name="convglu_ffn",
    )(x1, mod, pre, wg, wv, cw, cb, wd, pn)


def _rotate_half_axial(t):
    def rh(u):
        a, b = jnp.split(u, 2, axis=-1)
        return jnp.concatenate([-b, a], axis=-1)
    tr, tc = jnp.split(t, 2, axis=-1)
    return jnp.concatenate([rh(tr), rh(tc)], axis=-1)


def _rope_tables(seq_len):
    n_rows = seq_len // GRID_W
    row = jnp.repeat(jnp.arange(n_rows), GRID_W).astype(f32)
    col = jnp.tile(jnp.arange(GRID_W), n_rows).astype(f32)
    axis_dim = QK_ROPE_DIM // 2
    inv_freq = ROPE_THETA ** (-jnp.arange(0, axis_dim, 2, dtype=f32) / axis_dim)
    ang_r = row[:, None] * inv_freq
    ang_c = col[:, None] * inv_freq
    ang = jnp.concatenate([ang_r, ang_r, ang_c, ang_c], axis=-1)
    return jnp.cos(ang), jnp.sin(ang)


def _group_lanes(v):
    v = v.reshape(2, SSD_GROUPS, HEADS_PER_GROUP).transpose(1, 0, 2).reshape(SSD_GROUPS, 2 * HEADS_PER_GROUP)
    v = jnp.pad(v, ((0, 0), (0, LANES - 2 * HEADS_PER_GROUP)))
    return v.reshape(1, SSD_GROUPS * LANES)


def kernel(x, c, ctx, c_ctx, w_mod, b_mod, mix_pre_norm, mix_post_norm, w_in, q_norm, w_q_up, kv_norm, w_kv_up, ssd_conv_w, ssd_conv_b, ssd_a_log, ssd_dt_bias, ssd_d, ssd_norm, w_out, ffn_pre_norm, ffn_post_norm, w_up, ffn_conv_w, ffn_conv_b, w_down):
    bsz, seq, _ = x.shape
    assert w_mod.shape[0] == 1, "single-layer stack: the context stream is never updated"
    l = 0
    row2 = lambda v: v.reshape(1, -1)

    mod_rows = -(-(bsz + 1) // 8) * 8
    cvec = jnp.zeros((mod_rows, D_MODEL), f32).at[:bsz].set(c).at[bsz].set(c_ctx)

    wi = w_in[l]
    o_cq, o_ckv, o_kr, o_z, o_xbc, o_dt = np.cumsum([0, Q_LORA_RANK, KV_LORA_RANK, QK_ROPE_DIM, D_INNER, XBC_WIDTH])
    w_kr = wi[:, o_kr:o_z]
    w_kr_rot = _rotate_half_axial(w_kr)
    zpad = jnp.zeros((D_MODEL, QK_NOPE_DIM), f32)
    w_dt = wi[:, o_dt:].reshape(D_MODEL, 2, SSD_GROUPS, HEADS_PER_GROUP).transpose(0, 2, 1, 3)
    w_dt = jnp.pad(w_dt.reshape(D_MODEL, SSD_GROUPS, 2 * HEADS_PER_GROUP),
                   ((0, 0), (0, 0), (0, LANES - 2 * HEADS_PER_GROUP))).reshape(D_MODEL, SSD_GROUPS * LANES)
    w1 = jnp.concatenate([wi[:, o_cq:o_kr], zpad, w_kr, w_kr, zpad, w_kr_rot, w_kr_rot,
                          wi[:, o_z:o_dt], w_dt], axis=1).astype(bf16)
    assert w1.shape[1] == _C_END

    wq3 = w_q_up[l].reshape(Q_LORA_RANK, N_ATTN_HEADS, QK_NOPE_DIM + QK_ROPE_DIM)
    wq_rope = wq3[..., QK_NOPE_DIM:]
    wq = jnp.concatenate([wq3[..., :QK_NOPE_DIM], wq_rope, _rotate_half_axial(wq_rope)], axis=-1)
    wq = wq.reshape(Q_LORA_RANK, N_ATTN_HEADS * HEAD_PAD).astype(bf16)
    wkv3 = w_kv_up[l].reshape(KV_LORA_RANK, N_ATTN_HEADS, QK_NOPE_DIM + V_HEAD_DIM)
    wk = jnp.pad(wkv3[..., :QK_NOPE_DIM], ((0, 0), (0, 0), (0, HEAD_PAD - QK_NOPE_DIM)))
    wk = wk.reshape(KV_LORA_RANK, N_ATTN_HEADS * HEAD_PAD).astype(bf16)
    wv = wkv3[..., QK_NOPE_DIM:].reshape(KV_LORA_RANK, N_ATTN_HEADS * V_HEAD_DIM).astype(bf16)

    cos, sin = _rope_tables(seq)
    ones = jnp.ones((seq, QK_NOPE_DIM), f32)
    zeros = jnp.zeros((seq, QK_NOPE_DIM), f32)
    tabq = jnp.concatenate([ones, cos, sin], axis=1) * (ATTN_SCALE * math.log2(math.e))
    cosk = jnp.concatenate([zeros, cos, cos], axis=1)
    sink = jnp.concatenate([zeros, sin, sin], axis=1)

    dtb = _group_lanes(ssd_dt_bias[l])
    alog = _group_lanes(ssd_a_log[l])
    dexp = jnp.repeat(ssd_d[l], SSD_HEAD_DIM).reshape(1, D_INNER)

    wo = w_out[l].astype(bf16)
    wup = w_up[l].astype(bf16)
    wg, wval = wup[:, :D_FF], wup[:, D_FF:]
    wd = w_down[l].astype(bf16)

    mod = _modulation(cvec, w_mod[l].astype(bf16), row2(b_mod[l])).reshape(mod_rows, N_MOD, D_MODEL)
    pre = row2(mix_pre_norm[l])
    kvn = row2(kv_norm[l])
    q, kx, vx, z, xbc_x, dt_x = _inproj(x, mod, None, pre, w1, kvn, wk, wv,
                                       (row2(q_norm[l]), wq, tabq, cosk, sink), tm=512)
    kc, vc, xbc_c, dt_c = _inproj(ctx, mod, bsz, pre, w1, kvn, wk, wv, None, tm=ctx.shape[1])
    attn = _attention(q, kx, kc, vx, vc)
    y = _ssd(xbc_x, xbc_c, dt_x, dt_c, ssd_conv_w[l], row2(ssd_conv_b[l]), dtb, alog, dexp)
    x1 = _outproj(attn, y, z, x, mod, wo[:N_ATTN_HEADS * V_HEAD_DIM], wo[N_ATTN_HEADS * V_HEAD_DIM:],
                  row2(ssd_norm[l]), row2(mix_post_norm[l]), tm=1024)
    return _ffn(x1, mod, row2(ffn_pre_norm[l]), wg, wval, ffn_conv_w[l], row2(ffn_conv_b[l]), wd,
                row2(ffn_post_norm[l]), tf=256)
```

```python
import functools
import math

import jax
import jax.numpy as jnp
import numpy as np
from jax import lax
from jax.experimental import pallas as pl
from jax.experimental.pallas import tpu as pltpu

f32 = jnp.float32
bf16 = jnp.bfloat16

D_MODEL = 1024
GRID_W = 64
N_ATTN_HEADS = 16
QK_NOPE_DIM = 64
QK_ROPE_DIM = 32
V_HEAD_DIM = 64
Q_LORA_RANK = 384
KV_LORA_RANK = 256
ROPE_THETA = 10000.0
ATTN_SCALE = (QK_NOPE_DIM + QK_ROPE_DIM) ** -0.5
N_SSD_HEADS = 16
SSD_HEAD_DIM = 64
SSD_GROUPS = 2
HEADS_PER_GROUP = N_SSD_HEADS // SSD_GROUPS
SSD_STATE = 128
SSD_CONV = 5
SSD_CHUNK = 128
D_INNER = N_SSD_HEADS * SSD_HEAD_DIM
GN = SSD_GROUPS * SSD_STATE
XBC_WIDTH = D_INNER + 2 * GN
D_FF = 2816
FFN_CONV = 3
N_MOD = 6
EPS = 1e-6

LANES = 128
HEAD_PAD = 128
GROUP_W = HEADS_PER_GROUP * SSD_HEAD_DIM
VMEM_LIMIT = 56 * 1024 * 1024

_C_CQ = 0
_C_CKV = _C_CQ + Q_LORA_RANK
_C_KR = _C_CKV + KV_LORA_RANK
_C_Z = _C_KR + 2 * LANES
_C_XBC = _C_Z + D_INNER
_C_DT = _C_XBC + XBC_WIDTH
_C_END = _C_DT + SSD_GROUPS * LANES


def _cparams(sem, flags=None):
    return pltpu.CompilerParams(dimension_semantics=sem, vmem_limit_bytes=VMEM_LIMIT, flags=flags)


def _rms(v, w):
    return v * lax.rsqrt(jnp.mean(v * v, axis=-1, keepdims=True) + EPS) * w


def _sigmoid(v):
    return 1.0 / (1.0 + jnp.exp(-v))


def _dot(a, b):
    return jnp.dot(a, b, preferred_element_type=f32)


def _dot_nt(a, b):
    return lax.dot_general(a, b, (((1,), (1,)), ((), ())), preferred_element_type=f32)


def _mod_body(c_ref, w_ref, b_ref, o_ref):
    c = c_ref[...]
    s = (c * _sigmoid(c)).astype(bf16)
    o_ref[...] = _dot(s, w_ref[...]) + b_ref[...]


def _modulation(cvec, w_mod, b_mod):
    rows = cvec.shape[0]
    n = w_mod.shape[1]
    bn = D_MODEL
    return pl.pallas_call(
        _mod_body,
        out_shape=jax.ShapeDtypeStruct((rows, n), f32),
        grid=(n // bn,),
        in_specs=[pl.BlockSpec((rows, D_MODEL), lambda j: (0, 0)),
                  pl.BlockSpec((D_MODEL, bn), lambda j: (0, j)),
                  pl.BlockSpec((1, bn), lambda j: (0, j))],
        out_specs=pl.BlockSpec((rows, bn), lambda j: (0, j)),
        compiler_params=_cparams(("arbitrary",)),
        name="modulation",
    )(cvec, w_mod, b_mod)


def _inproj_body(*refs, is_ctx):
    if is_ctx:
        (x_ref, mod_ref, pre_ref, w1_ref, kvn_ref, wk_ref, wv_ref,
         k_ref, v_ref, xbc_ref, dt_ref) = refs
    else:
        (x_ref, mod_ref, pre_ref, w1_ref, kvn_ref, wk_ref, wv_ref, qn_ref, wq_ref,
         tabq_ref, cosk_ref, sink_ref,
         q_ref, k_ref, v_ref, z_ref, xbc_ref, dt_ref) = refs
    x = x_ref[0]
    shift = mod_ref[0, 0:1, :]
    scale = mod_ref[0, 1:2, :]
    h = (_rms(x, pre_ref[...]) * (1.0 + scale) + shift).astype(bf16)

    ckv = _dot(h, w1_ref[:, _C_CKV:_C_KR])
    ckvn = _rms(ckv, kvn_ref[...]).astype(bf16)
    v_ref[0] = _dot(ckvn, wv_ref[...]).astype(bf16)
    kres = _dot(ckvn, wk_ref[...])
    if is_ctx:
        kk = _dot(h, w1_ref[:, _C_KR:_C_KR + LANES])
    else:
        kr2 = _dot(h, w1_ref[:, _C_KR:_C_Z])
        kk = kr2[:, :LANES] * cosk_ref[...] + kr2[:, LANES:] * sink_ref[...]
    for hd in range(N_ATTN_HEADS):
        sl = slice(hd * HEAD_PAD, (hd + 1) * HEAD_PAD)
        k_ref[0, :, sl] = (kres[:, sl] + kk).astype(bf16)

    if not is_ctx:
        cq = _dot(h, w1_ref[:, _C_CQ:_C_CKV])
        cqn = _rms(cq, qn_ref[...]).astype(bf16)
        qres = _dot(cqn, wq_ref[...])
        tab = tabq_ref[...]
        for hd in range(N_ATTN_HEADS):
            sl = slice(hd * HEAD_PAD, (hd + 1) * HEAD_PAD)
            q_ref[0, :, sl] = (qres[:, sl] * tab).astype(bf16)
        z_ref[0] = _dot(h, w1_ref[:, _C_Z:_C_XBC]).astype(bf16)

    xbc_ref[0] = _dot(h, w1_ref[:, _C_XBC:_C_DT]).astype(bf16)
    dt_ref[0] = _dot(h, w1_ref[:, _C_DT:_C_END])


def _inproj(x, mod, mod_row, pre, w1, kvn, wk, wv, latent_args, tm):
    bsz, length, _ = x.shape
    is_ctx = latent_args is None
    nt = length // tm
    full = lambda a: pl.BlockSpec(a.shape, lambda b, i: (0,) * a.ndim)
    row = lambda w: pl.BlockSpec((1, tm, w), lambda b, i: (b, i, 0))
    tab = lambda a: pl.BlockSpec((tm, a.shape[1]), lambda b, i: (i, 0))
    if mod_row is None:
        mod_spec = pl.BlockSpec((1, N_MOD, D_MODEL), lambda b, i: (b, 0, 0))
    else:
        mod_spec = pl.BlockSpec((1, N_MOD, D_MODEL), lambda b, i: (mod_row, 0, 0))
    in_specs = [row(D_MODEL), mod_spec, full(pre), full(w1), full(kvn), full(wk), full(wv)]
    args = [x, mod, pre, w1, kvn, wk, wv]
    kw = N_ATTN_HEADS * HEAD_PAD
    vw = N_ATTN_HEADS * V_HEAD_DIM
    dtw = SSD_GROUPS * LANES
    sds = lambda w, dt: jax.ShapeDtypeStruct((bsz, length, w), dt)
    if is_ctx:
        out_shape = (sds(kw, bf16), sds(vw, bf16), sds(XBC_WIDTH, bf16), sds(dtw, f32))
        out_specs = (row(kw), row(vw), row(XBC_WIDTH), row(dtw))
    else:
        qn, wq, tabq, cosk, sink = latent_args
        in_specs += [full(qn), full(wq), tab(tabq), tab(cosk), tab(sink)]
        args += [qn, wq, tabq, cosk, sink]
        out_shape = (sds(kw, bf16), sds(kw, bf16), sds(vw, bf16), sds(D_INNER, bf16),
                     sds(XBC_WIDTH, bf16), sds(dtw, f32))
        out_specs = (row(kw), row(kw), row(vw), row(D_INNER), row(XBC_WIDTH), row(dtw))
    return pl.pallas_call(
        functools.partial(_inproj_body, is_ctx=is_ctx),
        out_shape=out_shape,
        grid=(bsz, nt),
        in_specs=in_specs,
        out_specs=out_specs,
        compiler_params=_cparams(("parallel", "parallel")),
        name="inproj_ctx" if is_ctx else "inproj_latent",
    )(*args)


_ATTN_TQ = 256
_ATTN_SLOTS = 5
_ATTN_PAIRS = 1
_ATTN_EXP_SPAN = 1.0


def _attn_body(q_ref, kx_ref, kc_ref, vx_ref, vc_ref, o_ref, *scratch):
    nslot = _ATTN_SLOTS
    s_bufs, m_bufs, p_bufs = scratch[:nslot], scratch[nslot:2 * nslot], scratch[2 * nslot:3 * nslot]
    vxa, vca = scratch[3 * nslot:]
    tq = _ATTN_TQ
    s = q_ref.shape[1]
    lc = kc_ref.shape[1]
    nt = s // tq
    lane_v = lax.broadcasted_iota(jnp.int32, (1, LANES), 1)
    own = [lane_v < V_HEAD_DIM, lane_v >= V_HEAD_DIM]
    npl = vx_ref.shape[2] // LANES
    for pi in range(npl):
        pl_lanes = slice(pi * LANES, (pi + 1) * LANES)
        for hh in range(2):
            vxa[2 * pi + hh] = jnp.where(own[hh], vx_ref[0, :, pl_lanes].astype(f32), 1.0).astype(bf16)
            vca[2 * pi + hh] = jnp.where(own[hh], vc_ref[0, :, pl_lanes].astype(f32), 1.0).astype(bf16)

    def rows_of(g):
        t = g % nt
        return slice(t * tq, (t + 1) * tq)

    kbw = lc
    nkb = (lc + s) // kbw

    def scores_items(t, slot):
        s_ref, m_ref = slot
        for hh in range(2):
            head = 2 * (t // nt) + hh
            sl = slice(head * HEAD_PAD, (head + 1) * HEAD_PAD)
            run = {}
            for kb in range(nkb):
                def item(tok, hh=hh, sl=sl, run=run, kb=kb):
                    q = q_ref[0, rows_of(t), sl]
                    kblk = kc_ref[0, :, sl] if kb == 0 else kx_ref[0, (kb - 1) * kbw:kb * kbw, sl]
                    blk = _dot_nt(q, kblk)
                    s_ref[hh, :, kb * kbw:(kb + 1) * kbw] = blk
                    part = blk[:, 0:LANES]
                    for c in range(LANES, kbw, LANES):
                        part = jnp.maximum(part, blk[:, c:c + LANES])
                    run["m"] = part if kb == 0 else jnp.maximum(run["m"], part)
                    if kb == nkb - 1:
                        m_ref[hh] = run["m"]
                    return blk[tq - 16:tq, kbw - LANES:kbw]
                yield item

    def exps_items(slot, p_ref):
        s_ref, m_ref = slot
        strip = 16
        for hh in range(2):
            for r in range(0, tq, strip):
                def item(tok, hh=hh, r=r):
                    m = jnp.max(m_ref[hh, r:r + strip, :], axis=-1, keepdims=True)
                    if tok is not None:
                        tie = tok[:, 0:1]
                        m = jnp.where(tie == tie, m, tie)
                    for c in range(0, lc + s, LANES):
                        d = s_ref[hh, r:r + strip, c:c + LANES] - m
                        p_ref[hh, r:r + strip, c:c + LANES] = jnp.exp2(d.astype(bf16))
                yield item

    def outputs_items(t, p_ref, res):
        for hh in range(2):
            run = {}
            head = 2 * (t // nt) + hh
            for kb in range(nkb):
                def item(tok, hh=hh, run=run, kb=kb, head=head):
                    vblk = vca[head] if kb == 0 else vxa[head, (kb - 1) * kbw:kb * kbw, :]
                    o = _dot(p_ref[hh, :, kb * kbw:(kb + 1) * kbw], vblk)
                    run["o"] = o if kb == 0 else run["o"] + o
                    if kb == nkb - 1:
                        o = run["o"]
                        den = o[:, V_HEAD_DIM:V_HEAD_DIM + 1] if hh == 0 else o[:, 0:1]
                        res.append(o / den)
                    return o[tq - 16:tq, :]
                yield item

    def run_interleaved(*streams, spans=None):
        items = []
        for si, stream in enumerate(streams):
            stream = list(stream)
            span = 1.0 if spans is None else spans[si]
            items += [(span * (i + 0.5) / len(stream), si, it) for i, it in enumerate(stream)]
        tok = None
        for _, _, it in sorted(items, key=lambda x: x[:2]):
            out = it(tok)
            tok = tok if out is None else out

    def store(t, res):
        pi = t // nt
        o_ref[0, rows_of(t), pi * LANES:(pi + 1) * LANES] = jnp.where(own[0], res[0], res[1]).astype(bf16)

    s_slot = tuple(zip(s_bufs, m_bufs))
    p_slot = tuple(p_bufs)
    run_interleaved(scores_items(0, s_slot[0]))
    run_interleaved(exps_items(s_slot[0], p_slot[0]), scores_items(1, s_slot[1]))

    ntot = npl * nt
    for t in range(2, ntot):
        res = []
        run_interleaved(outputs_items(t - 2, p_slot[(t - 2) % nslot], res),
                        exps_items(s_slot[(t - 1) % nslot], p_slot[(t - 1) % nslot]),
                        scores_items(t, s_slot[t % nslot]), spans=(1.0, _ATTN_EXP_SPAN, 1.0))
        store(t - 2, res)
        if t % (nslot - 1) == 0:
            pl.delay(1)
    res = []
    run_interleaved(outputs_items(ntot - 2, p_slot[(ntot - 2) % nslot], res),
                    exps_items(s_slot[(ntot - 1) % nslot], p_slot[(ntot - 1) % nslot]))
    store(ntot - 2, res)
    res = []
    run_interleaved(outputs_items(ntot - 1, p_slot[(ntot - 1) % nslot], res))
    store(ntot - 1, res)


def _attention(q, kx, kc, vx, vc):
    bsz, s, _ = q.shape
    lc = kc.shape[1]
    npl = _ATTN_PAIRS
    nheads = 2 * npl
    assert N_ATTN_HEADS % nheads == 0
    tq = _ATTN_TQ
    assert s % tq == 0 and s // tq >= 2
    seq = lambda length, w: pl.BlockSpec((1, length, w), lambda b, p: (b, 0, p))
    return pl.pallas_call(
        _attn_body,
        out_shape=jax.ShapeDtypeStruct((bsz, s, N_ATTN_HEADS * V_HEAD_DIM), bf16),
        grid=(bsz, N_ATTN_HEADS // nheads),
        in_specs=[seq(s, nheads * HEAD_PAD), seq(s, nheads * HEAD_PAD), seq(lc, nheads * HEAD_PAD),
                  seq(s, nheads * V_HEAD_DIM), seq(lc, nheads * V_HEAD_DIM)],
        out_specs=seq(s, nheads * V_HEAD_DIM),
        scratch_shapes=[pltpu.VMEM((2, tq, lc + s), f32)] * _ATTN_SLOTS
                       + [pltpu.VMEM((2, tq, LANES), f32)] * _ATTN_SLOTS
                       + [pltpu.VMEM((2, tq, lc + s), bf16)] * _ATTN_SLOTS + [
                        pltpu.VMEM((nheads, s, LANES), bf16), pltpu.VMEM((nheads, lc, LANES), bf16)],
        compiler_params=_cparams(("parallel", "parallel")),
        name="attention",
    )(q, kx, kc, vx, vc)


_HALO = 16


def _conv_silu_pass(src_ref, w_ref, b_ref, dst_ref, length):
    q = SSD_CHUNK
    nc = length // q
    half = SSD_CONV // 2
    w = w_ref[...]
    bias = b_ref[...]

    def step(c, carry):
        t0 = pl.multiple_of(c * q, q)
        tp = pl.multiple_of(jnp.maximum(t0 - _HALO, 0), _HALO)
        tn = pl.multiple_of(jnp.minimum(t0 + q, length - _HALO), _HALO)
        prev = src_ref[0, pl.ds(tp, _HALO), :].astype(f32)
        nxt = src_ref[0, pl.ds(tn, _HALO), :].astype(f32)
        prev = jnp.where(c > 0, prev, 0.0)
        nxt = jnp.where(c < nc - 1, nxt, 0.0)
        win = jnp.concatenate([prev, src_ref[0, pl.ds(t0, q), :].astype(f32), nxt], axis=0)
        sub = 8
        win = win[_HALO - sub:_HALO + q + sub, :]
        n = q + 2 * sub
        acc = bias
        for k in range(SSD_CONV):
            shift = (half - k) % n
            tap = win if shift == 0 else pltpu.roll(win, shift, 0)
            acc = acc + w[k:k + 1, :] * tap[sub:sub + q, :]
        dst_ref[pl.ds(t0, q), :] = acc * _sigmoid(acc)
        return carry

    lax.fori_loop(0, nc, step, 0)


def _softplus(v):
    return jnp.maximum(v, 0.0) + jnp.log1p(jnp.exp(-jnp.abs(v)))


def _split3_dot(m_bf, a):
    a1 = a.astype(bf16)
    r1 = a - a1.astype(f32)
    a2 = r1.astype(bf16)
    a3 = (r1 - a2.astype(f32)).astype(bf16)
    return _dot(m_bf, a1) + _dot(m_bf, a2) + _dot(m_bf, a3)


def _split2(a):
    a1 = a.astype(bf16)
    a2 = (a - a1.astype(f32)).astype(bf16)
    return jnp.concatenate([a1, a2], axis=1)


def _ssd_chunk(xs, bm, cm, dt_raw, dt_bias, a_row, st_ref, sel_head, sel_half, d, out):
    want_y = out is not None
    q = SSD_CHUNK
    ri = lax.broadcasted_iota(jnp.int32, (q, q), 0)
    ci = lax.broadcasted_iota(jnp.int32, (q, q), 1)
    mask = (ci <= ri) if d == 0 else (ci >= ri)
    first_half = lax.broadcasted_iota(jnp.int32, (q, LANES), 1) < SSD_HEAD_DIM
    edge = q - 1 if d == 0 else 0

    dt = _softplus(dt_raw + dt_bias)
    a = dt * a_row
    tri = jnp.where(mask, 1.0, 0.0).astype(bf16)
    acum = _split3_dot(tri, a)
    yield
    a2 = _split2(acum)
    dtb = _dot(_split2(dt), sel_half)
    bt = bm.T.astype(bf16)
    if want_y:
        colb = _dot(a2, sel_head)
        at = acum.T
        cb = _dot_nt(cm.astype(bf16), bm.astype(bf16))
    else:
        col_half = _dot(a2, sel_half)
    yield
    ys = []
    for p in range(HEADS_PER_GROUP // 2):
        psl = slice(p * LANES, (p + 1) * LANES)
        xd = xs[:, psl] * dtb[:, psl]
        if want_y:
            lhs, cols = [], []
            for e in (2 * p, 2 * p + 1):
                j = d * HEADS_PER_GROUP + e
                col = colb[:, e * LANES:(e + 1) * LANES]
                seg = jnp.exp2(jnp.where(mask, col - at[j:j + 1, :], -jnp.inf))
                lhs.append(jnp.concatenate([(cb * seg).astype(bf16),
                                            (cm * jnp.exp2(col)).astype(bf16)], axis=1))
                cols.append(col)
            col2 = jnp.where(first_half, cols[0], cols[1])
        else:
            col2 = col_half[:, psl]
        tot = col2[edge:edge + 1, :]
        xdec = (xd * jnp.exp2(tot - col2)).astype(bf16)
        yield
        st = st_ref[:, psl]
        if want_y:
            rhs = jnp.concatenate([xd.astype(bf16), st.astype(bf16)], axis=0)
            ys.append(jnp.where(first_half, _dot(lhs[0], rhs), _dot(lhs[1], rhs)))
        st_ref[:, psl] = jnp.exp2(tot) * st + _dot(bt, xdec)
        yield
    if want_y:
        out.append(jnp.concatenate(ys, axis=1))


def _lockstep(gens):
    gens = list(gens)
    while gens:
        for g in list(gens):
            try:
                next(g)
            except StopIteration:
                gens.remove(g)


_SSD_UNROLL = 1


def _ssd_body(xs_x, b_x, c_x, xs_c, b_c, c_c, dt_x, dt_c, cw_xs, cw_b, cw_c, cb_xs, cb_b, cb_c,
              dtb_ref, alog_ref, dexp_ref, selh_ref, selq_ref, y_ref,
              xs_s, b_s, c_s, xsc_s, bc_s, cc_s, y_s, stf_ref, stb_ref):
    s = xs_x.shape[1]
    lc = xs_c.shape[1]
    q = SSD_CHUNK
    nc, ncc = s // q, lc // q
    _conv_silu_pass(xs_x, cw_xs, cb_xs, xs_s, s)
    _conv_silu_pass(b_x, cw_b, cb_b, b_s, s)
    _conv_silu_pass(c_x, cw_c, cb_c, c_s, s)
    _conv_silu_pass(xs_c, cw_xs, cb_xs, xsc_s, lc)
    _conv_silu_pass(b_c, cw_b, cb_b, bc_s, lc)
    _conv_silu_pass(c_c, cw_c, cb_c, cc_s, lc)

    dt_bias = dtb_ref[...]
    a_row = -jnp.exp(alog_ref[...]) * math.log2(math.e)
    dexp = dexp_ref[...]
    st_refs = (stf_ref, stb_ref)
    for st_ref in st_refs:
        st_ref[...] = jnp.zeros(st_ref.shape, f32)

    def chunk_rows(i, n, d):
        c = i if d == 0 else n - 1 - i
        return pl.ds(pl.multiple_of(c * q, q), q)

    def scan_step(i, n, xs_r, b_r, c_r, dt_r, with_y, unroll):
        jobs = []
        for u in range(unroll):
            for d in range(2):
                rows = chunk_rows(i * unroll + u, n, d)
                out = [] if with_y else None
                gen = _ssd_chunk(xs_r[rows, :], b_r[rows, :], c_r[rows, :], dt_r[0, rows, :],
                                 dt_bias, a_row, st_refs[d], selh_ref[d], selq_ref[d], d, out)
                jobs.append((rows, out, gen))
        _lockstep(g for _, _, g in jobs)
        if with_y:
            for rows, out, _ in jobs:
                y_s[rows, :] = y_s[rows, :] + out[0]

    def ctx_step(i, carry):
        scan_step(i, ncc, xsc_s, bc_s, cc_s, dt_c, False, 1)
        return carry

    lax.fori_loop(0, ncc, ctx_step, 0)

    def skip_step(c, carry):
        rows = pl.ds(pl.multiple_of(c * q, q), q)
        y_s[rows, :] = dexp * xs_s[rows, :]
        return carry

    lax.fori_loop(0, nc, skip_step, 0)

    def lat_step(i, carry):
        scan_step(i, nc, xs_s, b_s, c_s, dt_x, True, _SSD_UNROLL)
        return carry

    assert nc % _SSD_UNROLL == 0
    lax.fori_loop(0, nc // _SSD_UNROLL, lat_step, 0)

    def out_step(c, carry):
        rows = pl.ds(pl.multiple_of(c * q, q), q)
        y_ref[0, rows, :] = y_s[rows, :].astype(bf16)
        return carry

    lax.fori_loop(0, nc, out_step, 0)


def _lane_select(width):
    sel = np.zeros((2, 2 * LANES, HEADS_PER_GROUP * width), np.float32)
    for d in range(2):
        for k in range(2):
            for e in range(HEADS_PER_GROUP):
                sel[d, k * LANES + d * HEADS_PER_GROUP + e, e * width:(e + 1) * width] = 1.0
    return jnp.asarray(sel, dtype=bf16)


def _ssd(xbc_x, xbc_c, dt_x, dt_c, conv_w, conv_b, dtb, alog, dexp):
    bsz, s, _ = xbc_x.shape
    lc = xbc_c.shape[1]
    selh = _lane_select(LANES)
    selq = _lane_select(SSD_HEAD_DIM)
    nxb = D_INNER // GROUP_W
    b_blk = D_INNER // SSD_STATE
    c_blk = (D_INNER + GN) // SSD_STATE
    seq = lambda length, w, blk: pl.BlockSpec((1, length, w), lambda b, g: (b, 0, blk(g)))
    par = lambda rows, w, blk: pl.BlockSpec((rows, w), lambda b, g: (0, blk(g)))
    xs_i = lambda g: g
    b_i = lambda g: b_blk + g
    c_i = lambda g: c_blk + g
    in_specs = [seq(s, GROUP_W, xs_i), seq(s, SSD_STATE, b_i), seq(s, SSD_STATE, c_i),
                seq(lc, GROUP_W, xs_i), seq(lc, SSD_STATE, b_i), seq(lc, SSD_STATE, c_i),
                seq(s, LANES, xs_i), seq(lc, LANES, xs_i),
                par(SSD_CONV, GROUP_W, xs_i), par(SSD_CONV, SSD_STATE, b_i), par(SSD_CONV, SSD_STATE, c_i),
                par(1, GROUP_W, xs_i), par(1, SSD_STATE, b_i), par(1, SSD_STATE, c_i),
                par(1, LANES, xs_i), par(1, LANES, xs_i), par(1, GROUP_W, xs_i),
                pl.BlockSpec(selh.shape, lambda b, g: (0, 0, 0)),
                pl.BlockSpec(selq.shape, lambda b, g: (0, 0, 0))]
    assert nxb == SSD_GROUPS
    scratch = [pltpu.VMEM((s, GROUP_W), f32), pltpu.VMEM((s, SSD_STATE), f32), pltpu.VMEM((s, SSD_STATE), f32),
               pltpu.VMEM((lc, GROUP_W), f32), pltpu.VMEM((lc, SSD_STATE), f32), pltpu.VMEM((lc, SSD_STATE), f32),
               pltpu.VMEM((s, GROUP_W), f32),
               pltpu.VMEM((SSD_STATE, GROUP_W), f32), pltpu.VMEM((SSD_STATE, GROUP_W), f32)]
    return pl.pallas_call(
        _ssd_body,
        out_shape=jax.ShapeDtypeStruct((bsz, s, D_INNER), bf16),
        grid=(bsz, SSD_GROUPS),
        in_specs=in_specs,
        out_specs=pl.BlockSpec((1, s, GROUP_W), lambda b, g: (b, 0, g)),
        scratch_shapes=scratch,
        compiler_params=_cparams(("parallel", "parallel")),
        name="ssd_scan",
    )(xbc_x, xbc_x, xbc_x, xbc_c, xbc_c, xbc_c, dt_x, dt_c, conv_w, conv_w, conv_w,
      conv_b, conv_b, conv_b, dtb, alog, dexp, selh, selq)


def _outproj_body(attn_ref, y_ref, z_ref, x_ref, mod_ref, wa_ref, ws_ref, sn_ref, pn_ref, o_ref):
    z = z_ref[0].astype(f32)
    g = y_ref[0].astype(f32) * (z * _sigmoid(z))
    gn = _rms(g, sn_ref[...]).astype(bf16)
    mix = _dot(attn_ref[0], wa_ref[...]) + _dot(gn, ws_ref[...])
    o_ref[0] = x_ref[0] + mod_ref[0, 2:3, :] * _rms(mix, pn_ref[...])


def _outproj(attn, y, z, x, mod, wa, ws, sn, pn, tm):
    bsz, s, _ = x.shape
    full = lambda a: pl.BlockSpec(a.shape, lambda b, i: (0,) * a.ndim)
    row = lambda w: pl.BlockSpec((1, tm, w), lambda b, i: (b, i, 0))
    return pl.pallas_call(
        _outproj_body,
        out_shape=jax.ShapeDtypeStruct((bsz, s, D_MODEL), f32),
        grid=(bsz, s // tm),
        in_specs=[row(attn.shape[2]), row(D_INNER), row(D_INNER), row(D_MODEL),
                  pl.BlockSpec((1, N_MOD, D_MODEL), lambda b, i: (b, 0, 0)),
                  full(wa), full(ws), full(sn), full(pn)],
        out_specs=row(D_MODEL),
        compiler_params=_cparams(("parallel", "parallel")),
        name="outproj",
    )(attn, y, z, x, mod, wa, ws, sn, pn)


_FFN_ROWS = 512
_FFN_PAD = 8


def _ffn_body(x_ref, mod_ref, pre_ref, wg_ref, wv_ref, cw_ref, cb_ref, wd_ref, pn_ref, o_ref, h_s, g_s):
    j = pl.program_id(1)
    nj = pl.num_programs(1)
    s = x_ref.shape[1]
    tf = wg_ref.shape[1]
    rb = _FFN_ROWS
    nrb = s // rb

    @pl.when(j == 0)
    def _():
        shift = mod_ref[0, 3:4, :]
        scale = mod_ref[0, 4:5, :]
        pre = pre_ref[...]

        def hstep(r, carry):
            rows = pl.ds(pl.multiple_of(r * rb, rb), rb)
            h_s[rows, :] = (_rms(x_ref[0, rows, :], pre) * (1.0 + scale) + shift).astype(bf16)
            o_ref[0, rows, :] = jnp.zeros((rb, D_MODEL), f32)
            return carry

        lax.fori_loop(0, nrb, hstep, 0)
        g_s[0:_FFN_PAD, :] = jnp.zeros((_FFN_PAD, tf), f32)
        g_s[_FFN_PAD + s:2 * _FFN_PAD + s, :] = jnp.zeros((_FFN_PAD, tf), f32)

    for r in range(nrb):
        r0 = r * rb
        g_s[r0 + _FFN_PAD:r0 + _FFN_PAD + rb, :] = _dot(h_s[r0:r0 + rb, :], wg_ref[...])

    cw = cw_ref[...]
    cb = cb_ref[...]
    for r in range(nrb):
        r0 = r * rb
        win = g_s[r0:r0 + rb + 2 * _FFN_PAD, :]
        gc = cb + cw[0:1, :] * win[_FFN_PAD - 1:_FFN_PAD - 1 + rb, :]
        gc = gc + cw[1:2, :] * win[_FFN_PAD:_FFN_PAD + rb, :]
        gc = gc + cw[2:3, :] * win[_FFN_PAD + 1:_FFN_PAD + 1 + rb, :]
        val = _dot(h_s[r0:r0 + rb, :], wv_ref[...])
        act = (0.5 * gc * (1.0 + lax.erf(gc * (1.0 / math.sqrt(2.0)))) * val).astype(bf16)
        o_ref[0, r0:r0 + rb, :] = o_ref[0, r0:r0 + rb, :] + _dot(act, wd_ref[...])

    @pl.when(j == nj - 1)
    def _():
        gate = mod_ref[0, 5:6, :]
        pn = pn_ref[...]

        def fstep(r, carry):
            rows = pl.ds(pl.multiple_of(r * rb, rb), rb)
            o_ref[0, rows, :] = x_ref[0, rows, :] + gate * _rms(o_ref[0, rows, :], pn)
            return carry

        lax.fori_loop(0, nrb, fstep, 0)


def _ffn(x1, mod, pre, wg, wv, cw, cb, wd, pn, tf):
    bsz, s, _ = x1.shape
    nj = D_FF // tf
    return pl.pallas_call(
        _ffn_body,
        out_shape=jax.ShapeDtypeStruct((bsz, s, D_MODEL), f32),
        grid=(bsz, nj),
        in_specs=[pl.BlockSpec((1, s, D_MODEL), lambda b, j: (b, 0, 0)),
                  pl.BlockSpec((1, N_MOD, D_MODEL), lambda b, j: (b, 0, 0)),
                  pl.BlockSpec((1, D_MODEL), lambda b, j: (0, 0)),
                  pl.BlockSpec((D_MODEL, tf), lambda b, j: (0, j)),
                  pl.BlockSpec((D_MODEL, tf), lambda b, j: (0, j)),
                  pl.BlockSpec((FFN_CONV, tf), lambda b, j: (0, j)),
                  pl.BlockSpec((1, tf), lambda b, j: (0, j)),
                  pl.BlockSpec((tf, D_MODEL), lambda b, j: (j, 0)),
                  pl.BlockSpec((1, D_MODEL), lambda b, j: (0, 0))],
        out_specs=pl.BlockSpec((1, s, D_MODEL), lambda b, j: (b, 0, 0)),
        scratch_shapes=[pltpu.VMEM((s, D_MODEL), bf16), pltpu.VMEM((s + 2 * _FFN_PAD, tf), f32)],
        compiler_params=_cparams(("parallel", "arbitrary")),
        name="convglu_ffn",
    )(x1, mod, pre, wg, wv, cw, cb, wd, pn)


def _rotate_half_axial(t):
    def rh(u):
        a, b = jnp.split(u, 2, axis=-1)
        return jnp.concatenate([-b, a], axis=-1)
    tr, tc = jnp.split(t, 2, axis=-1)
    return jnp.concatenate([rh(tr), rh(tc)], axis=-1)


def _rope_tables(seq_len):
    n_rows = seq_len // GRID_W
    row = jnp.repeat(jnp.arange(n_rows), GRID_W).astype(f32)
    col = jnp.tile(jnp.arange(GRID_W), n_rows).astype(f32)
    axis_dim = QK_ROPE_DIM // 2
    inv_freq = ROPE_THETA ** (-jnp.arange(0, axis_dim, 2, dtype=f32) / axis_dim)
    ang_r = row[:, None] * inv_freq
    ang_c = col[:, None] * inv_freq
    ang = jnp.concatenate([ang_r, ang_r, ang_c, ang_c], axis=-1)
    return jnp.cos(ang), jnp.sin(ang)


def _group_lanes(v):
    v = v.reshape(2, SSD_GROUPS, HEADS_PER_GROUP).transpose(1, 0, 2).reshape(SSD_GROUPS, 2 * HEADS_PER_GROUP)
    v = jnp.pad(v, ((0, 0), (0, LANES - 2 * HEADS_PER_GROUP)))
    return v.reshape(1, SSD_GROUPS * LANES)


def kernel(x, c, ctx, c_ctx, w_mod, b_mod, mix_pre_norm, mix_post_norm, w_in, q_norm, w_q_up, kv_norm, w_kv_up, ssd_conv_w, ssd_conv_b, ssd_a_log, ssd_dt_bias, ssd_d, ssd_norm, w_out, ffn_pre_norm, ffn_post_norm, w_up, ffn_conv_w, ffn_conv_b, w_down):
    bsz, seq, _ = x.shape
    assert w_mod.shape[0] == 1, "single-layer stack: the context stream is never updated"
    l = 0
    row2 = lambda v: v.reshape(1, -1)

    mod_rows = -(-(bsz + 1) // 8) * 8
    cvec = jnp.zeros((mod_rows, D_MODEL), f32).at[:bsz].set(c).at[bsz].set(c_ctx)

    wi = w_in[l]
    o_cq, o_ckv, o_kr, o_z, o_xbc, o_dt = np.cumsum([0, Q_LORA_RANK, KV_LORA_RANK, QK_ROPE_DIM, D_INNER, XBC_WIDTH])
    w_kr = wi[:, o_kr:o_z]
    w_kr_rot = _rotate_half_axial(w_kr)
    zpad = jnp.zeros((D_MODEL, QK_NOPE_DIM), f32)
    w_dt = wi[:, o_dt:].reshape(D_MODEL, 2, SSD_GROUPS, HEADS_PER_GROUP).transpose(0, 2, 1, 3)
    w_dt = jnp.pad(w_dt.reshape(D_MODEL, SSD_GROUPS, 2 * HEADS_PER_GROUP),
                   ((0, 0), (0, 0), (0, LANES - 2 * HEADS_PER_GROUP))).reshape(D_MODEL, SSD_GROUPS * LANES)
    w1 = jnp.concatenate([wi[:, o_cq:o_kr], zpad, w_kr, w_kr, zpad, w_kr_rot, w_kr_rot,
                          wi[:, o_z:o_dt], w_dt], axis=1).astype(bf16)
    assert w1.shape[1] == _C_END

    wq3 = w_q_up[l].reshape(Q_LORA_RANK, N_ATTN_HEADS, QK_NOPE_DIM + QK_ROPE_DIM)
    wq_rope = wq3[..., QK_NOPE_DIM:]
    wq = jnp.concatenate([wq3[..., :QK_NOPE_DIM], wq_rope, _rotate_half_axial(wq_rope)], axis=-1)
    wq = wq.reshape(Q_LORA_RANK, N_ATTN_HEADS * HEAD_PAD).astype(bf16)
    wkv3 = w_kv_up[l].reshape(KV_LORA_RANK, N_ATTN_HEADS, QK_NOPE_DIM + V_HEAD_DIM)
    wk = jnp.pad(wkv3[..., :QK_NOPE_DIM], ((0, 0), (0, 0), (0, HEAD_PAD - QK_NOPE_DIM)))
    wk = wk.reshape(KV_LORA_RANK, N_ATTN_HEADS * HEAD_PAD).astype(bf16)
    wv = wkv3[..., QK_NOPE_DIM:].reshape(KV_LORA_RANK, N_ATTN_HEADS * V_HEAD_DIM).astype(bf16)

    cos, sin = _rope_tables(seq)
    ones = jnp.ones((seq, QK_NOPE_DIM), f32)
    zeros = jnp.zeros((seq, QK_NOPE_DIM), f32)
    tabq = jnp.concatenate([ones, cos, sin], axis=1) * (ATTN_SCALE * math.log2(math.e))
    cosk = jnp.concatenate([zeros, cos, cos], axis=1)
    sink = jnp.concatenate([zeros, sin, sin], axis=1)

    dtb = _group_lanes(ssd_dt_bias[l])
    alog = _group_lanes(ssd_a_log[l])
    dexp = jnp.repeat(ssd_d[l], SSD_HEAD_DIM).reshape(1, D_INNER)

    wo = w_out[l].astype(bf16)
    wup = w_up[l].astype(bf16)
    wg, wval = wup[:, :D_FF], wup[:, D_FF:]
    wd = w_down[l].astype(bf16)

    mod = _modulation(cvec, w_mod[l].astype(bf16), row2(b_mod[l])).reshape(mod_rows, N_MOD, D_MODEL)
    pre = row2(mix_pre_norm[l])
    kvn = row2(kv_norm[l])
    q, kx, vx, z, xbc_x, dt_x = _inproj(x, mod, None, pre, w1, kvn, wk, wv,
                                       (row2(q_norm[l]), wq, tabq, cosk, sink), tm=512)
    kc, vc, xbc_c, dt_c = _inproj(ctx, mod, bsz, pre, w1, kvn, wk, wv, None, tm=ctx.shape[1])
    attn = _attention(q, kx, kc, vx, vc)
    y = _ssd(xbc_x, xbc_c, dt_x, dt_c, ssd_conv_w[l], row2(ssd_conv_b[l]), dtb, alog, dexp)
    x1 = _outproj(attn, y, z, x, mod, wo[:N_ATTN_HEADS * V_HEAD_DIM], wo[N_ATTN_HEADS * V_HEAD_DIM:],
                  row2(ssd_norm[l]), row2(mix_post_norm[l]), tm=1024)
    return _ffn(x1, mod, row2(ffn_pre_norm[l]), wg, wval, ffn_conv_w[l], row2(ffn_conv_b[l]), wd,
                row2(ffn_post_norm[l]), tf=256)
```
